```python
import jax, jax.numpy as jnp
from jax import lax
import numpy as np

D_MODEL = 1024
BATCH = 8
SEQ = 4096
DEPTH = 2

MIX_WIDTH = D_MODEL
POOL_WIDTH = MIX_WIDTH // 2
POOL_GROUPS = 4
POOL_GROUP_DIM = POOL_WIDTH // POOL_GROUPS
POOL_WINDOWS = (2, 4, 8, 16)
CONV_WIDTH = MIX_WIDTH - POOL_WIDTH
CONV_HEADS = 8
CONV_HEAD_DIM = CONV_WIDTH // CONV_HEADS
CONV_KERNEL = 31
IN_PROJ_WIDTH = POOL_WIDTH + 2 * CONV_WIDTH
N_GROUPS = 4
EXPERTS_PER_GROUP = 8
N_EXPERTS = N_GROUPS * EXPERTS_PER_GROUP
TOP_K = 2
EXPERT_HIDDEN = D_MODEL // 4
N_MOD = 6
RMS_EPS = 1e-6
LN_EPS = 1e-5

kernel_name = "hybrid_pool_conformer_hmoe_adaln"


def rmsnorm(x, g):
    xf = x.astype(jnp.float32)
    y = xf * lax.rsqrt(jnp.mean(jnp.square(xf), axis=-1, keepdims=True) + RMS_EPS)
    return (y * g.astype(jnp.float32)).astype(x.dtype)


def pool_mixer(u, pool_w, pool_scale):
    b, s, _ = u.shape
    uf = u.astype(jnp.float32).reshape(b, s, POOL_GROUPS, POOL_GROUP_DIM)
    cs = lax.cumsum(uf, axis=1)
    pos = jnp.arange(s)
    outs = []
    for g, w in enumerate(POOL_WINDOWS):
        cg = cs[:, :, g]
        lag = jnp.pad(cg[:, :s - w], ((0, 0), (w, 0), (0, 0)))
        count = jnp.minimum(pos + 1, w).astype(jnp.float32)[None, :, None]
        outs.append((cg - lag) / count - uf[:, :, g])
    pooled = jnp.stack(outs, axis=2).astype(u.dtype)
    y = jnp.einsum('bsgc,gcd->bsgd', pooled, pool_w).reshape(b, s, POOL_WIDTH)
    return y * pool_scale


def conv_module(a, gate, conv_w, conv_b, ln_g, ln_b):
    v = a * jax.nn.sigmoid(gate)
    y = lax.conv_general_dilated(
        v, conv_w[:, None, :], window_strides=(1,), padding=[(CONV_KERNEL - 1, 0)],
        dimension_numbers=('NWC', 'WIO', 'NWC'), feature_group_count=CONV_WIDTH) + conv_b
    b, s, _ = y.shape
    yf = y.astype(jnp.float32).reshape(b, s, CONV_HEADS, CONV_HEAD_DIM)
    mu = jnp.mean(yf, axis=-1, keepdims=True)
    var = jnp.mean(jnp.square(yf - mu), axis=-1, keepdims=True)
    yn = ((yf - mu) * lax.rsqrt(var + LN_EPS)).reshape(b, s, CONV_WIDTH)
    yn = yn * ln_g.astype(jnp.float32) + ln_b.astype(jnp.float32)
    return jax.nn.silu(yn).astype(a.dtype)


def hier_moe(h, rg_w, rg_b, re_w, re_b, w_gate, w_up, w_down):
    b, s, d = h.shape
    t = h.reshape(b * s, d)
    n = t.shape[0]
    g_logits = (t @ rg_w).astype(jnp.float32) + rg_b.astype(jnp.float32)
    g_prob = jax.nn.softmax(g_logits, axis=-1)
    g_sel = jnp.argmax(g_logits, axis=-1)
    p_group = jnp.take_along_axis(g_prob, g_sel[:, None], axis=-1)
    e_logits = ((t @ re_w).astype(jnp.float32) + re_b.astype(jnp.float32)).reshape(n, N_GROUPS, EXPERTS_PER_GROUP)
    e_logits = jnp.take_along_axis(e_logits, g_sel[:, None, None], axis=1)[:, 0]
    top_logit, top_idx = lax.top_k(e_logits, TOP_K)
    weights = p_group * jax.nn.softmax(top_logit, axis=-1)
    eid = g_sel[:, None] * EXPERTS_PER_GROUP + top_idx
    gates = jnp.zeros((n, N_EXPERTS), jnp.float32).at[jnp.arange(n)[:, None], eid].add(weights).astype(h.dtype)
    out = jnp.zeros_like(t)
    for e in range(N_EXPERTS):
        hid = jax.nn.silu(t @ w_gate[e]) * (t @ w_up[e])
        out = out + gates[:, e:e + 1] * (hid @ w_down[e])
    return out.reshape(b, s, d)


def setup_inputs(seed: int = 0) -> dict:
    key = jax.random.key(seed)
    ks = jax.random.split(key, 24)
    f32 = jnp.float32
    nrm = lambda k, shape, scale: (jax.random.normal(k, shape, f32) * scale).astype(f32)
    L, D = DEPTH, D_MODEL
    return {
        "x": nrm(ks[0], (BATCH, SEQ, D), 1.0),
        "c": nrm(ks[1], (BATCH, D), 1.0),
        "ada_w": nrm(ks[2], (L, D, N_MOD * D), 0.5 * D ** -0.5),
        "ada_b": nrm(ks[3], (L, N_MOD * D), 0.02),
        "norm1_g": 1.0 + nrm(ks[4], (L, D), 0.02),
        "w_in": nrm(ks[5], (L, D, IN_PROJ_WIDTH), D ** -0.5),
        "pool_w": nrm(ks[6], (L, POOL_GROUPS, POOL_GROUP_DIM, POOL_GROUP_DIM), POOL_GROUP_DIM ** -0.5),
        "pool_scale": 1.0 + nrm(ks[7], (L, POOL_WIDTH), 0.1),
        "conv_w": nrm(ks[8], (L, CONV_KERNEL, CONV_WIDTH), CONV_KERNEL ** -0.5),
        "conv_b": nrm(ks[9], (L, CONV_WIDTH), 0.02),
        "conv_ln_g": 1.0 + nrm(ks[10], (L, CONV_WIDTH), 0.02),
        "conv_ln_b": nrm(ks[11], (L, CONV_WIDTH), 0.02),
        "w_out": nrm(ks[12], (L, MIX_WIDTH, D), MIX_WIDTH ** -0.5),
        "norm2_g": 1.0 + nrm(ks[13], (L, D), 0.02),
        "router_group_w": nrm(ks[14], (L, D, N_GROUPS), D ** -0.5),
        "router_group_b": nrm(ks[15], (L, N_GROUPS), 0.01),
        "router_expert_w": nrm(ks[16], (L, D, N_EXPERTS), D ** -0.5),
        "router_expert_b": nrm(ks[17], (L, N_EXPERTS), 0.01),
        "expert_w_gate": nrm(ks[18], (L, N_EXPERTS, D, EXPERT_HIDDEN), D ** -0.5),
        "expert_w_up": nrm(ks[19], (L, N_EXPERTS, D, EXPERT_HIDDEN), D ** -0.5),
        "expert_w_down": nrm(ks[20], (L, N_EXPERTS, EXPERT_HIDDEN, D), EXPERT_HIDDEN ** -0.5),
        "final_g": 1.0 + nrm(ks[21], (D,), 0.02),
    }


def reference(x, c, ada_w, ada_b, norm1_g, w_in, pool_w, pool_scale, conv_w, conv_b,
              conv_ln_g, conv_ln_b, w_out, norm2_g, router_group_w, router_group_b,
              router_expert_w, router_expert_b, expert_w_gate, expert_w_up, expert_w_down,
              final_g):
    cond = jax.nn.silu(c)
    for l in range(DEPTH):
        mod = (cond @ ada_w[l] + ada_b[l])[:, None, :]
        sh1, sc1, g1, sh2, sc2, g2 = jnp.split(mod, N_MOD, axis=-1)
        h = rmsnorm(x, norm1_g[l]) * (1.0 + sc1) + sh1
        z = h @ w_in[l]
        u_pool = z[..., :POOL_WIDTH]
        glu_a = z[..., POOL_WIDTH:POOL_WIDTH + CONV_WIDTH]
        glu_g = z[..., POOL_WIDTH + CONV_WIDTH:]
        y_pool = pool_mixer(u_pool, pool_w[l], pool_scale[l])
        y_conv = conv_module(glu_a, glu_g, conv_w[l], conv_b[l], conv_ln_g[l], conv_ln_b[l])
        y = jnp.concatenate([y_pool, y_conv], axis=-1) @ w_out[l]
        x = x + g1 * y
        h = rmsnorm(x, norm2_g[l]) * (1.0 + sc2) + sh2
        x = x + g2 * hier_moe(h, router_group_w[l], router_group_b[l], router_expert_w[l],
                              router_expert_b[l], expert_w_gate[l], expert_w_up[l], expert_w_down[l])
    return rmsnorm(x, final_g)
```

```python
import functools

import jax
import jax.numpy as jnp
from jax import lax
from jax.experimental import pallas as pl
from jax.experimental.pallas import tpu as pltpu

D_MODEL = 1024
POOL_WIDTH = 512
POOL_GROUPS = 4
POOL_GROUP_DIM = 128
POOL_WINDOWS = (2, 4, 8, 16)
CONV_WIDTH = 512
CONV_HEAD_DIM = 64
CONV_KERNEL = 31
N_GROUPS = 4
EXPERTS_PER_GROUP = 8
N_EXPERTS = 32
EXPERT_HIDDEN = 256
N_MOD = 6
RMS_EPS = 1e-6
LN_EPS = 1e-5

LANES = 128
SUBLANES = 8
POOL_HALO = 16
CONV_HALO = 32
MIX_TILE = 512
CONV_ROWS = 32
LN_BLOCK = 256
MOE_TILE = 1024
VMEM_LIMIT = 56 * 1024 * 1024

F32 = jnp.float32
BF16 = jnp.bfloat16


def _silu(v):
    return v * jax.nn.sigmoid(v)


def _split_bf16(v):
    hi = v.astype(BF16)
    lo = (v - hi.astype(F32)).astype(BF16)
    return hi, lo


def _adaln_kernel(c_ref, w_ref, b_ref, o_ref):
    cond = _silu(c_ref[...])
    o_ref[...] = jnp.dot(cond, w_ref[...], preferred_element_type=F32,
                         precision=lax.Precision.HIGHEST) + b_ref[...]


def _adaln(c, ada_w, ada_b):
    depth, d, n = ada_w.shape
    b = c.shape[0]
    tn = D_MODEL
    return pl.pallas_call(
        _adaln_kernel,
        grid=(depth, n // tn),
        in_specs=[
            pl.BlockSpec((b, d), lambda l, j: (0, 0)),
            pl.BlockSpec((None, d, tn), lambda l, j: (l, 0, j)),
            pl.BlockSpec((None, 1, tn), lambda l, j: (l, 0, j)),
        ],
        out_specs=pl.BlockSpec((None, b, tn), lambda l, j: (l, 0, j)),
        out_shape=jax.ShapeDtypeStruct((depth, b, n), F32),
        compiler_params=pltpu.CompilerParams(dimension_semantics=("arbitrary", "arbitrary")),
        name="adaln",
    )(c, ada_w, ada_b.reshape(depth, 1, n))


def _route(logits):
    t = logits.shape[0]
    lane = lax.broadcasted_iota(jnp.int32, (t, LANES), 1)
    lanef = lane.astype(F32)
    neg = float("-inf")
    big = float(4 * LANES)
    is_group = (lane >= N_EXPERTS) & (lane < N_EXPERTS + N_GROUPS)
    gl = jnp.where(is_group, logits, neg)
    gmax = jnp.max(gl, axis=-1, keepdims=True)
    gidx = jnp.min(jnp.where(gl == gmax, lanef, big), axis=-1, keepdims=True)
    p_group = 1.0 / jnp.sum(jnp.exp(gl - gmax), axis=-1, keepdims=True)
    gsel = gidx.astype(jnp.int32) - N_EXPERTS
    in_group = (lane < N_EXPERTS) & ((lane >> 3) == gsel)
    el = jnp.where(in_group, logits, neg)
    m1 = jnp.max(el, axis=-1, keepdims=True)
    i1 = jnp.min(jnp.where(el == m1, lanef, big), axis=-1, keepdims=True)
    el2 = jnp.where(lanef == i1, neg, el)
    m2 = jnp.max(el2, axis=-1, keepdims=True)
    i2 = jnp.min(jnp.where(el2 == m2, lanef, big), axis=-1, keepdims=True)
    ratio = jnp.exp(m2 - m1)
    w1 = p_group / (1.0 + ratio)
    w2 = p_group * ratio / (1.0 + ratio)
    return jnp.where(lanef == i1, w1, 0.0) + jnp.where(lanef == i2, w2, 0.0)


def _mixer_kernel(x_ref, mod_ref, n1g_ref, win_ref, poolw_ref, pscale_ref, convw_ref, convb_ref,
                  lng_ref, lnb_ref, avg_ref, wout_ref, n2g_ref, wrh_ref, wrl_ref, br_ref,
                  xo_ref, h2_ref, gates_ref,
                  uext, vsh, cbuf, ycat):
    t = MIX_TILE
    j = pl.program_id(1)

    @pl.when(j == 0)
    def _():
        uext[0:POOL_HALO, :] = jnp.zeros((POOL_HALO, POOL_WIDTH), F32)
        vsh[0, 0:CONV_HALO, :] = jnp.zeros((CONV_HALO, CONV_WIDTH), F32)

    @pl.when(j > 0)
    def _():
        uext[0:POOL_HALO, :] = uext[t:t + POOL_HALO, :]
        vsh[0, 0:CONV_HALO, :] = vsh[0, t:t + CONV_HALO, :]

    x = x_ref[...]
    shift1, scale1, gate1 = mod_ref[0:1, :], mod_ref[1:2, :], mod_ref[2:3, :]
    shift2, scale2, gate2 = mod_ref[3:4, :], mod_ref[4:5, :], mod_ref[5:6, :]
    del gate2

    ms = jnp.mean(x * x, axis=-1, keepdims=True)
    h = (x * lax.rsqrt(ms + RMS_EPS) * n1g_ref[...]) * (1.0 + scale1) + shift1
    hb = h.astype(BF16)

    uext[POOL_HALO:POOL_HALO + t, :] = jnp.dot(hb, win_ref[:, 0:POOL_WIDTH], preferred_element_type=F32)
    glu_a = jnp.dot(hb, win_ref[:, POOL_WIDTH:POOL_WIDTH + CONV_WIDTH], preferred_element_type=F32)
    glu_g = jnp.dot(hb, win_ref[:, POOL_WIDTH + CONV_WIDTH:], preferred_element_type=F32)
    vsh[0, CONV_HALO:CONV_HALO + t, :] = glu_a * jax.nn.sigmoid(glu_g)

    pos = j * t + lax.broadcasted_iota(jnp.int32, (t, 1), 0)
    for g, w in enumerate(POOL_WINDOWS):
        c0 = g * POOL_GROUP_DIM
        cur = uext[POOL_HALO:POOL_HALO + t, c0:c0 + POOL_GROUP_DIM]
        s = cur
        for d in range(1, w):
            s = s + uext[POOL_HALO - d:POOL_HALO - d + t, c0:c0 + POOL_GROUP_DIM]
        inv_count = 1.0 / jnp.minimum(pos + 1, w).astype(F32)
        pooled = s * inv_count - cur
        yp = jnp.dot(pooled.astype(BF16), poolw_ref[g], preferred_element_type=F32)
        ycat[:, c0:c0 + POOL_GROUP_DIM] = (yp * pscale_ref[:, c0:c0 + POOL_GROUP_DIM]).astype(BF16)

    shifted_rows = CONV_HALO + t - SUBLANES
    for r in range(1, SUBLANES):
        vsh[r, 0:shifted_rows, :] = vsh[0, r:r + shifted_rows, :]

    def conv_chunk(c, carry):
        r0 = pl.multiple_of(c * CONV_ROWS, CONV_ROWS)
        acc = jnp.broadcast_to(convb_ref[...], (CONV_ROWS, CONV_WIDTH))
        for k in range(CONV_KERNEL):
            q, r = divmod(CONV_HALO - (CONV_KERNEL - 1) + k, SUBLANES)
            acc = acc + convw_ref[k:k + 1, :] * vsh[r, pl.ds(r0 + q * SUBLANES, CONV_ROWS), :]
        cbuf[pl.ds(r0, CONV_ROWS), :] = acc
        return carry

    lax.fori_loop(0, t // CONV_ROWS, conv_chunk, 0)

    for c in range(CONV_WIDTH // LN_BLOCK):
        c0 = c * LN_BLOCK
        yc = cbuf[:, c0:c0 + LN_BLOCK]
        hi, lo = _split_bf16(yc)
        mu = (jnp.dot(hi, avg_ref[...], preferred_element_type=F32)
              + jnp.dot(lo, avg_ref[...], preferred_element_type=F32))
        dlt = yc - mu
        hi, lo = _split_bf16(dlt * dlt)
        var = (jnp.dot(hi, avg_ref[...], preferred_element_type=F32)
               + jnp.dot(lo, avg_ref[...], preferred_element_type=F32))
        yn = dlt * lax.rsqrt(var + LN_EPS) * lng_ref[:, c0:c0 + LN_BLOCK] + lnb_ref[:, c0:c0 + LN_BLOCK]
        ycat[:, POOL_WIDTH + c0:POOL_WIDTH + c0 + LN_BLOCK] = _silu(yn).astype(BF16)

    y = jnp.dot(ycat[...], wout_ref[...], preferred_element_type=F32)
    x1 = x + gate1 * y
    xo_ref[...] = x1

    ms2 = jnp.mean(x1 * x1, axis=-1, keepdims=True)
    h2 = (x1 * lax.rsqrt(ms2 + RMS_EPS) * n2g_ref[...]) * (1.0 + scale2) + shift2
    h2_ref[...] = h2.astype(BF16)
    hi, lo = _split_bf16(h2)
    logits = (jnp.dot(hi, wrh_ref[...], preferred_element_type=F32)
              + jnp.dot(lo, wrh_ref[...], preferred_element_type=F32)
              + jnp.dot(hi, wrl_ref[...], preferred_element_type=F32)) + br_ref[...]
    gates_ref[...] = _route(logits)


def _mixer(x, mod, n1g, win, poolw, pscale, convw, convb, lng, lnb, avg, wout, n2g, wrh, wrl, br):
    b, s, d = x.shape
    t = MIX_TILE
    const = lambda shape: pl.BlockSpec(shape, lambda i, j: (0,) * len(shape))
    tok = lambda width: pl.BlockSpec((None, t, width), lambda i, j: (i, j, 0))
    return pl.pallas_call(
        _mixer_kernel,
        grid=(b, s // t),
        in_specs=[
            tok(d),
            pl.BlockSpec((None, N_MOD, d), lambda i, j: (i, 0, 0)),
            const((1, d)),
            const(win.shape), const(poolw.shape), const((1, POOL_WIDTH)),
            const(convw.shape), const((1, CONV_WIDTH)), const((1, CONV_WIDTH)), const((1, CONV_WIDTH)),
            const(avg.shape), const(wout.shape), const((1, d)),
            const(wrh.shape), const(wrl.shape), const((1, LANES)),
        ],
        out_specs=[tok(d), tok(d), tok(LANES)],
        out_shape=[
            jax.ShapeDtypeStruct((b, s, d), F32),
            jax.ShapeDtypeStruct((b, s, d), BF16),
            jax.ShapeDtypeStruct((b, s, LANES), F32),
        ],
        scratch_shapes=[
            pltpu.VMEM((POOL_HALO + t, POOL_WIDTH), F32),
            pltpu.VMEM((SUBLANES, CONV_HALO + t, CONV_WIDTH), F32),
            pltpu.VMEM((t, CONV_WIDTH), F32),
            pltpu.VMEM((t, d), BF16),
        ],
        compiler_params=pltpu.CompilerParams(
            dimension_semantics=("arbitrary", "arbitrary"), vmem_limit_bytes=VMEM_LIMIT),
        name="mixer",
    )(x, mod, n1g, win, poolw, pscale, convw, convb, lng, lnb, avg, wout, n2g, wrh, wrl, br)


def _moe_kernel(h_ref, gates_ref, wg_ref, wu_ref, wd_ref, x_ref, mod_ref, fg_ref, o_ref, acc, *, final_norm):
    e = pl.program_id(1)

    @pl.when(e == 0)
    def _():
        acc[...] = jnp.zeros_like(acc)

    h = h_ref[...]
    hid = _silu(jnp.dot(h, wg_ref[...], preferred_element_type=F32)) * jnp.dot(
        h, wu_ref[...], preferred_element_type=F32)
    lane = lax.broadcasted_iota(jnp.int32, gates_ref.shape, 1)
    gate = jnp.sum(jnp.where(lane == e, gates_ref[...], 0.0), axis=-1, keepdims=True)
    acc[...] += jnp.dot((hid * gate).astype(BF16), wd_ref[...], preferred_element_type=F32)

    @pl.when(e == N_EXPERTS - 1)
    def _():
        x2 = x_ref[...] + mod_ref[5:6, :] * acc[...]
        if final_norm:
            ms = jnp.mean(x2 * x2, axis=-1, keepdims=True)
            x2 = x2 * lax.rsqrt(ms + RMS_EPS) * fg_ref[...]
        o_ref[...] = x2


def _moe(h2, gates, wg, wu, wd, x1, mod, final_g, final_norm):
    b, s, d = x1.shape
    t = MOE_TILE
    n = b * s
    per_seq = s // t
    return pl.pallas_call(
        functools.partial(_moe_kernel, final_norm=final_norm),
        grid=(n // t, N_EXPERTS),
        in_specs=[
            pl.BlockSpec((t, d), lambda i, e: (i, 0)),
            pl.BlockSpec((t, LANES), lambda i, e: (i, 0)),
            pl.BlockSpec((None, d, EXPERT_HIDDEN), lambda i, e: (e, 0, 0)),
            pl.BlockSpec((None, d, EXPERT_HIDDEN), lambda i, e: (e, 0, 0)),
            pl.BlockSpec((None, EXPERT_HIDDEN, d), lambda i, e: (e, 0, 0)),
            pl.BlockSpec((t, d), lambda i, e: (i, 0)),
            pl.BlockSpec((None, N_MOD, d), lambda i, e: (i // per_seq, 0, 0)),
            pl.BlockSpec((1, d), lambda i, e: (0, 0)),
        ],
        out_specs=pl.BlockSpec((t, d), lambda i, e: (i, 0)),
        out_shape=jax.ShapeDtypeStruct((n, d), F32),
        scratch_shapes=[pltpu.VMEM((t, d), F32)],
        compiler_params=pltpu.CompilerParams(
            dimension_semantics=("arbitrary", "arbitrary"), vmem_limit_bytes=VMEM_LIMIT),
        name="moe",
    )(h2.reshape(n, d), gates.reshape(n, LANES), wg, wu, wd, x1.reshape(n, d), mod, final_g).reshape(b, s, d)


def _router_weights(rg_w, rg_b, re_w, re_b):
    d = rg_w.shape[0]
    pad = LANES - N_EXPERTS - N_GROUPS
    w = jnp.concatenate([re_w, rg_w, jnp.zeros((d, pad), F32)], axis=1)
    bias = jnp.concatenate([re_b, rg_b, jnp.zeros((pad,), F32)]).reshape(1, LANES)
    w_hi = w.astype(BF16)
    w_lo = (w - w_hi.astype(F32)).astype(BF16)
    return w_hi, w_lo, bias


def kernel(x, c, ada_w, ada_b, norm1_g, w_in, pool_w, pool_scale, conv_w, conv_b, conv_ln_g, conv_ln_b,
           w_out, norm2_g, router_group_w, router_group_b, router_expert_w, router_expert_b,
           expert_w_gate, expert_w_up, expert_w_down, final_g):
    depth = ada_w.shape[0]
    b = x.shape[0]
    mods = _adaln(c, ada_w, ada_b).reshape(depth, b, N_MOD, D_MODEL)
    head = lax.broadcasted_iota(jnp.int32, (LN_BLOCK, LN_BLOCK), 0) // CONV_HEAD_DIM
    head_t = lax.broadcasted_iota(jnp.int32, (LN_BLOCK, LN_BLOCK), 1) // CONV_HEAD_DIM
    avg = jnp.where(head == head_t, 1.0 / CONV_HEAD_DIM, 0.0).astype(BF16)
    row = lambda v: v.reshape(1, -1)
    for l in range(depth):
        wrh, wrl, br = _router_weights(router_group_w[l], router_group_b[l],
                                       router_expert_w[l], router_expert_b[l])
        x1, h2, gates = _mixer(
            x, mods[l], row(norm1_g[l]), w_in[l].astype(BF16), pool_w[l].astype(BF16), row(pool_scale[l]),
            conv_w[l], row(conv_b[l]), row(conv_ln_g[l]), row(conv_ln_b[l]), avg,
            w_out[l].astype(BF16), row(norm2_g[l]), wrh, wrl, br)
        x = _moe(h2, gates, expert_w_gate[l].astype(BF16), expert_w_up[l].astype(BF16),
                 expert_w_down[l].astype(BF16), x1, mods[l], row(final_g), final_norm=(l == depth - 1))
    return x
```

```python
import functools

import jax
import jax.numpy as jnp
from jax import lax
from jax.experimental import pallas as pl
from jax.experimental.pallas import tpu as pltpu

D_MODEL = 1024
POOL_WIDTH = 512
POOL_GROUPS = 4
POOL_GROUP_DIM = 128
POOL_WINDOWS = (2, 4, 8, 16)
CONV_WIDTH = 512
CONV_HEAD_DIM = 64
CONV_KERNEL = 31
N_GROUPS = 4
EXPERTS_PER_GROUP = 8
N_EXPERTS = 32
EXPERT_HIDDEN = 256
N_MOD = 6
RMS_EPS = 1e-6
LN_EPS = 1e-5

LANES = 128
SUBLANES = 8
POOL_HALO = 16
CONV_HALO = 32
MIX_TILE = 512
CONV_ROWS = 32
LN_BLOCK = 256
MOE_SUB = 512
EXPERTS_PER_STEP = 2
EXPERT_TILE = 128
ROW_UNROLL = 8
VMEM_LIMIT = 56 * 1024 * 1024
MOE_VMEM_LIMIT = 60 * 1024 * 1024

F32 = jnp.float32
BF16 = jnp.bfloat16


def _silu(v):
    return v * jax.nn.sigmoid(v)


def _split_bf16(v):
    hi = v.astype(BF16)
    lo = (v - hi.astype(F32)).astype(BF16)
    return hi, lo


def _adaln_kernel(c_ref, w_ref, b_ref, o_ref):
    cond = _silu(c_ref[...])
    o_ref[...] = jnp.dot(cond, w_ref[...], preferred_element_type=F32,
                         precision=lax.Precision.HIGHEST) + b_ref[...]


def _adaln(c, ada_w, ada_b):
    depth, d, n = ada_w.shape
    b = c.shape[0]
    tn = D_MODEL
    return pl.pallas_call(
        _adaln_kernel,
        grid=(depth, n // tn),
        in_specs=[
            pl.BlockSpec((b, d), lambda l, j: (0, 0)),
            pl.BlockSpec((None, d, tn), lambda l, j: (l, 0, j)),
            pl.BlockSpec((None, 1, tn), lambda l, j: (l, 0, j)),
        ],
        out_specs=pl.BlockSpec((None, b, tn), lambda l, j: (l, 0, j)),
        out_shape=jax.ShapeDtypeStruct((depth, b, n), F32),
        compiler_params=pltpu.CompilerParams(dimension_semantics=("arbitrary", "arbitrary")),
        name="adaln",
    )(c, ada_w, ada_b.reshape(depth, 1, n))


def _route(logits, tril_ref, cnt_scr):
    t = logits.shape[0]
    lane = lax.broadcasted_iota(jnp.int32, (t, LANES), 1)
    lanef = lane.astype(F32)
    neg = float("-inf")
    big = float(4 * LANES)
    is_group = (lane >= N_EXPERTS) & (lane < N_EXPERTS + N_GROUPS)
    gl = jnp.where(is_group, logits, neg)
    gmax = jnp.max(gl, axis=-1, keepdims=True)
    gidx = jnp.min(jnp.where(gl == gmax, lanef, big), axis=-1, keepdims=True)
    p_group = 1.0 / jnp.sum(jnp.exp(gl - gmax), axis=-1, keepdims=True)
    gsel = gidx.astype(jnp.int32) - N_EXPERTS
    in_group = (lane < N_EXPERTS) & ((lane >> 3) == gsel)
    el = jnp.where(in_group, logits, neg)
    m1 = jnp.max(el, axis=-1, keepdims=True)
    i1 = jnp.min(jnp.where(el == m1, lanef, big), axis=-1, keepdims=True)
    el2 = jnp.where(lanef == i1, neg, el)
    m2 = jnp.max(el2, axis=-1, keepdims=True)
    i2 = jnp.min(jnp.where(el2 == m2, lanef, big), axis=-1, keepdims=True)
    ratio = jnp.exp(m2 - m1)
    w1 = p_group / (1.0 + ratio)
    w2 = p_group * ratio / (1.0 + ratio)
    hot1 = lanef == i1
    hot2 = lanef == i2
    onehot = jnp.where(hot1, 1.0, 0.0) + jnp.where(hot2, 1.0, 0.0)
    before = jnp.dot(tril_ref[...], onehot.astype(BF16), preferred_element_type=F32) + cnt_scr[...]
    rank1 = jnp.sum(jnp.where(hot1, before, 0.0), axis=-1, keepdims=True)
    rank2 = jnp.sum(jnp.where(hot2, before, 0.0), axis=-1, keepdims=True)
    cnt_scr[...] = cnt_scr[...] + jnp.sum(onehot, axis=0, keepdims=True)
    slab = jnp.zeros((t, LANES), F32)
    for col, val in enumerate((i1, i2, w1, w2, rank1, rank2)):
        slab = jnp.where(lane == col, val, slab)
    return slab


def _mixer_kernel(x_ref, mod_ref, n1g_ref, win_ref, poolw_ref, pscale_ref, convw_ref, convb_ref,
                  lng_ref, lnb_ref, avg_ref, wout_ref, n2g_ref, wrh_ref, wrl_ref, br_ref, tril_ref,
                  xo_ref, h2_ref, route_ref, cnt_ref,
                  uext, vsh, cbuf, ycat, cnt_scr):
    t = MIX_TILE
    j = pl.program_id(1)

    @pl.when(j == 0)
    def _():
        cnt_scr[...] = jnp.zeros_like(cnt_scr)
        uext[0:POOL_HALO, :] = jnp.zeros((POOL_HALO, POOL_WIDTH), F32)
        vsh[0, 0:CONV_HALO, :] = jnp.zeros((CONV_HALO, CONV_WIDTH), F32)

    @pl.when(j > 0)
    def _():
        uext[0:POOL_HALO, :] = uext[t:t + POOL_HALO, :]
        vsh[0, 0:CONV_HALO, :] = vsh[0, t:t + CONV_HALO, :]

    x = x_ref[...]
    shift1, scale1, gate1 = mod_ref[0:1, :], mod_ref[1:2, :], mod_ref[2:3, :]
    shift2, scale2, gate2 = mod_ref[3:4, :], mod_ref[4:5, :], mod_ref[5:6, :]
    del gate2

    ms = jnp.mean(x * x, axis=-1, keepdims=True)
    h = (x * lax.rsqrt(ms + RMS_EPS) * n1g_ref[...]) * (1.0 + scale1) + shift1
    hb = h.astype(BF16)

    uext[POOL_HALO:POOL_HALO + t, :] = jnp.dot(hb, win_ref[:, 0:POOL_WIDTH], preferred_element_type=F32)
    glu_a = jnp.dot(hb, win_ref[:, POOL_WIDTH:POOL_WIDTH + CONV_WIDTH], preferred_element_type=F32)
    glu_g = jnp.dot(hb, win_ref[:, POOL_WIDTH + CONV_WIDTH:], preferred_element_type=F32)
    vsh[0, CONV_HALO:CONV_HALO + t, :] = glu_a * jax.nn.sigmoid(glu_g)

    pos = j * t + lax.broadcasted_iota(jnp.int32, (t, 1), 0)
    for g, w in enumerate(POOL_WINDOWS):
        c0 = g * POOL_GROUP_DIM
        cur = uext[POOL_HALO:POOL_HALO + t, c0:c0 + POOL_GROUP_DIM]
        s = cur
        for d in range(1, w):
            s = s + uext[POOL_HALO - d:POOL_HALO - d + t, c0:c0 + POOL_GROUP_DIM]
        inv_count = 1.0 / jnp.minimum(pos + 1, w).astype(F32)
        pooled = s * inv_count - cur
        yp = jnp.dot(pooled.astype(BF16), poolw_ref[g], preferred_element_type=F32)
        ycat[:, c0:c0 + POOL_GROUP_DIM] = (yp * pscale_ref[:, c0:c0 + POOL_GROUP_DIM]).astype(BF16)

    shifted_rows = CONV_HALO + t - SUBLANES
    for r in range(1, SUBLANES):
        vsh[r, 0:shifted_rows, :] = vsh[0, r:r + shifted_rows, :]

    def conv_chunk(c, carry):
        r0 = pl.multiple_of(c * CONV_ROWS, CONV_ROWS)
        acc = jnp.broadcast_to(convb_ref[...], (CONV_ROWS, CONV_WIDTH))
        for k in range(CONV_KERNEL):
            q, r = divmod(CONV_HALO - (CONV_KERNEL - 1) + k, SUBLANES)
            acc = acc + convw_ref[k:k + 1, :] * vsh[r, pl.ds(r0 + q * SUBLANES, CONV_ROWS), :]
        cbuf[pl.ds(r0, CONV_ROWS), :] = acc
        return carry

    lax.fori_loop(0, t // CONV_ROWS, conv_chunk, 0)

    for c in range(CONV_WIDTH // LN_BLOCK):
        c0 = c * LN_BLOCK
        yc = cbuf[:, c0:c0 + LN_BLOCK]
        hi, lo = _split_bf16(yc)
        mu = (jnp.dot(hi, avg_ref[...], preferred_element_type=F32)
              + jnp.dot(lo, avg_ref[...], preferred_element_type=F32))
        dlt = yc - mu
        hi, lo = _split_bf16(dlt * dlt)
        var = (jnp.dot(hi, avg_ref[...], preferred_element_type=F32)
               + jnp.dot(lo, avg_ref[...], preferred_element_type=F32))
        yn = dlt * lax.rsqrt(var + LN_EPS) * lng_ref[:, c0:c0 + LN_BLOCK] + lnb_ref[:, c0:c0 + LN_BLOCK]
        ycat[:, POOL_WIDTH + c0:POOL_WIDTH + c0 + LN_BLOCK] = _silu(yn).astype(BF16)

    y = jnp.dot(ycat[...], wout_ref[...], preferred_element_type=F32)
    x1 = x + gate1 * y
    xo_ref[...] = x1

    ms2 = jnp.mean(x1 * x1, axis=-1, keepdims=True)
    h2 = (x1 * lax.rsqrt(ms2 + RMS_EPS) * n2g_ref[...]) * (1.0 + scale2) + shift2
    for s in range(SUBLANES):
        h2_ref[:, s, :] = h2[:, s * LANES:(s + 1) * LANES]
    hi, lo = _split_bf16(h2)
    logits = (jnp.dot(hi, wrh_ref[...], preferred_element_type=F32)
              + jnp.dot(lo, wrh_ref[...], preferred_element_type=F32)
              + jnp.dot(hi, wrl_ref[...], preferred_element_type=F32)) + br_ref[...]
    route_ref[...] = _route(logits, tril_ref, cnt_scr)
    cnt_ref[...] = cnt_scr[...]


def _mixer(x, mod, n1g, win, poolw, pscale, convw, convb, lng, lnb, avg, wout, n2g, wrh, wrl, br, tril):
    b, s, d = x.shape
    t = MIX_TILE
    assert d == SUBLANES * LANES
    const = lambda shape: pl.BlockSpec(shape, lambda i, j: (0,) * len(shape))
    tok = lambda width: pl.BlockSpec((None, t, width), lambda i, j: (i, j, 0))
    return pl.pallas_call(
        _mixer_kernel,
        grid=(b, s // t),
        in_specs=[
            tok(d),
            pl.BlockSpec((None, N_MOD, d), lambda i, j: (i, 0, 0)),
            const((1, d)),
            const(win.shape), const(poolw.shape), const((1, POOL_WIDTH)),
            const(convw.shape), const((1, CONV_WIDTH)), const((1, CONV_WIDTH)), const((1, CONV_WIDTH)),
            const(avg.shape), const(wout.shape), const((1, d)),
            const(wrh.shape), const(wrl.shape), const((1, LANES)), const(tril.shape),
        ],
        out_specs=[
            tok(d),
            pl.BlockSpec((None, t, SUBLANES, LANES), lambda i, j: (i, j, 0, 0)),
            tok(LANES),
            pl.BlockSpec((None, 1, LANES), lambda i, j: (i, 0, 0)),
        ],
        out_shape=[
            jax.ShapeDtypeStruct((b, s, d), F32),
            jax.ShapeDtypeStruct((b, s, SUBLANES, LANES), F32),
            jax.ShapeDtypeStruct((b, s, LANES), F32),
            jax.ShapeDtypeStruct((b, 1, LANES), F32),
        ],
        scratch_shapes=[
            pltpu.VMEM((POOL_HALO + t, POOL_WIDTH), F32),
            pltpu.VMEM((SUBLANES, CONV_HALO + t, CONV_WIDTH), F32),
            pltpu.VMEM((t, CONV_WIDTH), F32),
            pltpu.VMEM((t, d), BF16),
            pltpu.VMEM((1, LANES), F32),
        ],
        compiler_params=pltpu.CompilerParams(
            dimension_semantics=("arbitrary", "arbitrary"), vmem_limit_bytes=VMEM_LIMIT),
        name="mixer",
    )(x, mod, n1g, win, poolw, pscale, convw, convb, lng, lnb, avg, wout, n2g, wrh, wrl, br, tril)


def _moe_kernel(pos_ref, wts_ref, tbl_ref, h_ref, wgu_ref, wd_ref, x_ref, mod_ref, fg_ref, o_ref,
                gbuf, stage, *, final_norm, seq):
    nd = seq // MOE_SUB
    ne = N_EXPERTS // EXPERTS_PER_STEP
    b = pl.program_id(0)
    k = pl.program_id(1)
    base = b * (2 * seq)

    @pl.when(k == 0)
    def _():
        gbuf[pl.ds(2 * seq, EXPERT_TILE)] = jnp.zeros((EXPERT_TILE, SUBLANES, LANES), F32)

    @pl.when(k < nd)
    def _():
        t0 = k * MOE_SUB

        def body(i, carry):
            for u in range(ROW_UNROLL):
                t = i * ROW_UNROLL + u
                row = h_ref[t]
                gbuf[pos_ref[base + t0 + t]] = row
                gbuf[pos_ref[base + seq + t0 + t]] = row
            return carry

        lax.fori_loop(0, MOE_SUB // ROW_UNROLL, body, 0)

    @pl.when((k >= nd) & (k < nd + ne))
    def _():
        for j in range(EXPERTS_PER_STEP):
            e = (k - nd) * EXPERTS_PER_STEP + j
            start = tbl_ref[(b * N_EXPERTS + e) * 2]
            count = tbl_ref[(b * N_EXPERTS + e) * 2 + 1]

            def tile(i, carry, j=j, start=start, count=count):
                r0 = start + i * EXPERT_TILE
                xs = [gbuf[pl.ds(r0, EXPERT_TILE), s, :] for s in range(SUBLANES)]
                xb = jnp.concatenate(xs, axis=-1).astype(BF16)
                a = jnp.dot(xb, wgu_ref[j], preferred_element_type=F32)
                hid = _silu(a[:, :EXPERT_HIDDEN]) * a[:, EXPERT_HIDDEN:]
                y = jnp.dot(hid.astype(BF16), wd_ref[j], preferred_element_type=F32)
                valid = lax.broadcasted_iota(jnp.int32, (EXPERT_TILE, 1), 0) + i * EXPERT_TILE < count
                for s in range(SUBLANES):
                    gbuf[pl.ds(r0, EXPERT_TILE), s, :] = jnp.where(valid, y[:, s * LANES:(s + 1) * LANES], xs[s])
                return carry

            lax.fori_loop(0, (count + EXPERT_TILE - 1) // EXPERT_TILE, tile, 0)

    @pl.when(k >= nd + ne)
    def _():
        t0 = (k - nd - ne) * MOE_SUB

        def body(i, carry):
            for u in range(ROW_UNROLL):
                t = i * ROW_UNROLL + u
                y1 = gbuf[pos_ref[base + t0 + t]]
                y2 = gbuf[pos_ref[base + seq + t0 + t]]
                stage[t] = wts_ref[base + t0 + t] * y1 + wts_ref[base + seq + t0 + t] * y2
            return carry

        lax.fori_loop(0, MOE_SUB // ROW_UNROLL, body, 0)
        for s in range(SUBLANES):
            sl = slice(s * LANES, (s + 1) * LANES)
            o_ref[:, sl] = x_ref[:, sl] + mod_ref[5:6, sl] * stage[:, s, :]
        if final_norm:
            x2 = o_ref[...]
            ms = jnp.mean(x2 * x2, axis=-1, keepdims=True)
            o_ref[...] = x2 * lax.rsqrt(ms + RMS_EPS) * fg_ref[...]


def _moe(h2r, pos, wts, tbl, wgu, wd, x1, mod, final_g, final_norm):
    b, s, d = x1.shape
    nd = s // MOE_SUB
    ne = N_EXPERTS // EXPERTS_PER_STEP
    clip = lambda v, hi: jnp.minimum(jnp.maximum(v, 0), hi)
    tok_in = lambda i, k, *_: (i, clip(k, nd - 1), 0, 0)
    tok_out = lambda i, k, *_: (i, clip(k - nd - ne, nd - 1), 0)
    expert = lambda i, k, *_: (clip(k - nd, ne - 1), 0, 0)
    grid_spec = pltpu.PrefetchScalarGridSpec(
        num_scalar_prefetch=3,
        grid=(b, nd + ne + nd),
        in_specs=[
            pl.BlockSpec((None, MOE_SUB, SUBLANES, LANES), tok_in),
            pl.BlockSpec((EXPERTS_PER_STEP, d, 2 * EXPERT_HIDDEN), expert),
            pl.BlockSpec((EXPERTS_PER_STEP, EXPERT_HIDDEN, d), expert),
            pl.BlockSpec((None, MOE_SUB, d), tok_out),
            pl.BlockSpec((None, N_MOD, d), lambda i, k, *_: (i, 0, 0)),
            pl.BlockSpec((1, d), lambda i, k, *_: (0, 0)),
        ],
        out_specs=pl.BlockSpec((None, MOE_SUB, d), tok_out),
        scratch_shapes=[
            pltpu.VMEM((2 * s + EXPERT_TILE, SUBLANES, LANES), F32),
            pltpu.VMEM((MOE_SUB, SUBLANES, LANES), F32),
        ],
    )
    return pl.pallas_call(
        functools.partial(_moe_kernel, final_norm=final_norm, seq=s),
        grid_spec=grid_spec,
        out_shape=jax.ShapeDtypeStruct((b, s, d), F32),
        compiler_params=pltpu.CompilerParams(
            dimension_semantics=("arbitrary", "arbitrary"), vmem_limit_bytes=MOE_VMEM_LIMIT),
        name="moe",
    )(pos, wts, tbl, h2r, wgu, wd, x1, mod, final_g)


def _dispatch_tables(route, counts):
    b, s, _ = route.shape
    ids = route[..., 0:2].astype(jnp.int32)
    ranks = route[..., 4:6].astype(jnp.int32)
    cnt = counts[:, 0, :N_EXPERTS].astype(jnp.int32)
    starts = jnp.cumsum(cnt, axis=1) - cnt
    pos = jnp.take_along_axis(starts, ids.reshape(b, 2 * s), axis=1).reshape(b, s, 2) + ranks
    pos = jnp.swapaxes(pos, 1, 2).reshape(-1)
    wts = jnp.swapaxes(route[..., 2:4], 1, 2).reshape(-1)
    tbl = jnp.stack([starts, cnt], axis=-1).reshape(-1)
    return pos, wts, tbl


def _router_weights(rg_w, rg_b, re_w, re_b):
    d = rg_w.shape[0]
    pad = LANES - N_EXPERTS - N_GROUPS
    w = jnp.concatenate([re_w, rg_w, jnp.zeros((d, pad), F32)], axis=1)
    bias = jnp.concatenate([re_b, rg_b, jnp.zeros((pad,), F32)]).reshape(1, LANES)
    w_hi = w.astype(BF16)
    w_lo = (w - w_hi.astype(F32)).astype(BF16)
    return w_hi, w_lo, bias


def kernel(x, c, ada_w, ada_b, norm1_g, w_in, pool_w, pool_scale, conv_w, conv_b, conv_ln_g, conv_ln_b,
           w_out, norm2_g, router_group_w, router_group_b, router_expert_w, router_expert_b,
           expert_w_gate, expert_w_up, expert_w_down, final_g):
    depth = ada_w.shape[0]
    b = x.shape[0]
    mods = _adaln(c, ada_w, ada_b).reshape(depth, b, N_MOD, D_MODEL)
    head = lax.broadcasted_iota(jnp.int32, (LN_BLOCK, LN_BLOCK), 0) // CONV_HEAD_DIM
    head_t = lax.broadcasted_iota(jnp.int32, (LN_BLOCK, LN_BLOCK), 1) // CONV_HEAD_DIM
    avg = jnp.where(head == head_t, 1.0 / CONV_HEAD_DIM, 0.0).astype(BF16)
    earlier = (lax.broadcasted_iota(jnp.int32, (MIX_TILE, MIX_TILE), 1)
               < lax.broadcasted_iota(jnp.int32, (MIX_TILE, MIX_TILE), 0))
    tril = jnp.where(earlier, 1.0, 0.0).astype(BF16)
    row = lambda v: v.reshape(1, -1)
    for l in range(depth):
        wrh, wrl, br = _router_weights(router_group_w[l], router_group_b[l],
                                       router_expert_w[l], router_expert_b[l])
        x1, h2r, route, counts = _mixer(
            x, mods[l], row(norm1_g[l]), w_in[l].astype(BF16), pool_w[l].astype(BF16), row(pool_scale[l]),
            conv_w[l], row(conv_b[l]), row(conv_ln_g[l]), row(conv_ln_b[l]), avg,
            w_out[l].astype(BF16), row(norm2_g[l]), wrh, wrl, br, tril)
        pos, wts, tbl = _dispatch_tables(route, counts)
        wgu = jnp.concatenate([expert_w_gate[l], expert_w_up[l]], axis=-1).astype(BF16)
        x = _moe(h2r, pos, wts, tbl, wgu, expert_w_down[l].astype(BF16), x1, mods[l], row(final_g),
                 final_norm=(l == depth - 1))
    return x
```

```python
import functools

import jax
import jax.numpy as jnp
from jax import lax
from jax.experimental import pallas as pl
from jax.experimental.pallas import tpu as pltpu

D_MODEL = 1024
POOL_WIDTH = 512
POOL_GROUPS = 4
POOL_GROUP_DIM = 128
POOL_WINDOWS = (2, 4, 8, 16)
CONV_WIDTH = 512
CONV_HEAD_DIM = 64
CONV_KERNEL = 31
N_GROUPS = 4
EXPERTS_PER_GROUP = 8
N_EXPERTS = 32
EXPERT_HIDDEN = 256
N_MOD = 6
RMS_EPS = 1e-6
LN_EPS = 1e-5

LANES = 128
SUBLANES = 8
POOL_HALO = 16
CONV_HALO = 32
MIX_TILE = 512
CONV_ROWS = 32
LN_BLOCK = 256
MOE_SUB = 512
EXPERTS_PER_STEP = 2
EXPERT_TILE = 128
ROW_UNROLL = 8
VMEM_LIMIT = 56 * 1024 * 1024
MOE_VMEM_LIMIT = 60 * 1024 * 1024

F32 = jnp.float32
BF16 = jnp.bfloat16


def _silu(v):
    return v * jax.nn.sigmoid(v)


def _split_bf16(v):
    hi = v.astype(BF16)
    lo = (v - hi.astype(F32)).astype(BF16)
    return hi, lo


def _adaln_kernel(c_ref, w_ref, b_ref, o_ref):
    cond = _silu(c_ref[...])
    o_ref[...] = jnp.dot(cond, w_ref[...], preferred_element_type=F32,
                         precision=lax.Precision.HIGHEST) + b_ref[...]


def _adaln(c, ada_w, ada_b):
    depth, d, n = ada_w.shape
    b = c.shape[0]
    tn = D_MODEL
    return pl.pallas_call(
        _adaln_kernel,
        grid=(depth, n // tn),
        in_specs=[
            pl.BlockSpec((b, d), lambda l, j: (0, 0)),
            pl.BlockSpec((None, d, tn), lambda l, j: (l, 0, j)),
            pl.BlockSpec((None, 1, tn), lambda l, j: (l, 0, j)),
        ],
        out_specs=pl.BlockSpec((None, b, tn), lambda l, j: (l, 0, j)),
        out_shape=jax.ShapeDtypeStruct((depth, b, n), F32),
        compiler_params=pltpu.CompilerParams(dimension_semantics=("arbitrary", "arbitrary")),
        name="adaln",
    )(c, ada_w, ada_b.reshape(depth, 1, n))


def _route(logits, tril_ref, cnt_scr):
    t = logits.shape[0]
    lane = lax.broadcasted_iota(jnp.int32, (t, LANES), 1)
    lanef = lane.astype(F32)
    neg = float("-inf")
    big = float(4 * LANES)
    is_group = (lane >= N_EXPERTS) & (lane < N_EXPERTS + N_GROUPS)
    gl = jnp.where(is_group, logits, neg)
    gmax = jnp.max(gl, axis=-1, keepdims=True)
    gidx = jnp.min(jnp.where(gl == gmax, lanef, big), axis=-1, keepdims=True)
    p_group = 1.0 / jnp.sum(jnp.exp(gl - gmax), axis=-1, keepdims=True)
    gsel = gidx.astype(jnp.int32) - N_EXPERTS
    in_group = (lane < N_EXPERTS) & ((lane >> 3) == gsel)
    el = jnp.where(in_group, logits, neg)
    m1 = jnp.max(el, axis=-1, keepdims=True)
    i1 = jnp.min(jnp.where(el == m1, lanef, big), axis=-1, keepdims=True)
    el2 = jnp.where(lanef == i1, neg, el)
    m2 = jnp.max(el2, axis=-1, keepdims=True)
    i2 = jnp.min(jnp.where(el2 == m2, lanef, big), axis=-1, keepdims=True)
    ratio = jnp.exp(m2 - m1)
    w1 = p_group / (1.0 + ratio)
    w2 = p_group * ratio / (1.0 + ratio)
    hot1 = lanef == i1
    hot2 = lanef == i2
    onehot = jnp.where(hot1, 1.0, 0.0) + jnp.where(hot2, 1.0, 0.0)
    before = jnp.dot(tril_ref[...], onehot.astype(BF16), preferred_element_type=F32) + cnt_scr[...]
    rank1 = jnp.sum(jnp.where(hot1, before, 0.0), axis=-1, keepdims=True)
    rank2 = jnp.sum(jnp.where(hot2, before, 0.0), axis=-1, keepdims=True)
    cnt_scr[...] = cnt_scr[...] + jnp.sum(onehot, axis=0, keepdims=True)
    slab = jnp.zeros((t, LANES), F32)
    for col, val in enumerate((i1, i2, w1, w2, rank1, rank2)):
        slab = jnp.where(lane == col, val, slab)
    return slab


def _mixer_kernel(x_ref, mod_ref, n1g_ref, win_ref, poolw_ref, pscale_ref, convw_ref, convb_ref,
                  lng_ref, lnb_ref, avg_ref, wout_ref, n2g_ref, wrh_ref, wrl_ref, br_ref, tril_ref,
                  xo_ref, h2_ref, route_ref, cnt_ref,
                  uext, vsh, cbuf, ycat, cnt_scr):
    t = MIX_TILE
    j = pl.program_id(1)

    @pl.when(j == 0)
    def _():
        cnt_scr[...] = jnp.zeros_like(cnt_scr)
        uext[0:POOL_HALO, :] = jnp.zeros((POOL_HALO, POOL_WIDTH), F32)
        vsh[0, 0:CONV_HALO, :] = jnp.zeros((CONV_HALO, CONV_WIDTH), F32)

    @pl.when(j > 0)
    def _():
        uext[0:POOL_HALO, :] = uext[t:t + POOL_HALO, :]
        vsh[0, 0:CONV_HALO, :] = vsh[0, t:t + CONV_HALO, :]

    x = x_ref[...]
    shift1, scale1, gate1 = mod_ref[0:1, :], mod_ref[1:2, :], mod_ref[2:3, :]
    shift2, scale2, gate2 = mod_ref[3:4, :], mod_ref[4:5, :], mod_ref[5:6, :]
    del gate2

    ms = jnp.mean(x * x, axis=-1, keepdims=True)
    h = (x * lax.rsqrt(ms + RMS_EPS) * n1g_ref[...]) * (1.0 + scale1) + shift1
    hb = h.astype(BF16)

    uext[POOL_HALO:POOL_HALO + t, :] = jnp.dot(hb, win_ref[:, 0:POOL_WIDTH], preferred_element_type=F32)
    glu_a = jnp.dot(hb, win_ref[:, POOL_WIDTH:POOL_WIDTH + CONV_WIDTH], preferred_element_type=F32)
    glu_g = jnp.dot(hb, win_ref[:, POOL_WIDTH + CONV_WIDTH:], preferred_element_type=F32)
    vsh[0, CONV_HALO:CONV_HALO + t, :] = glu_a * jax.nn.sigmoid(glu_g)

    pos = j * t + lax.broadcasted_iota(jnp.int32, (t, 1), 0)
    for g, w in enumerate(POOL_WINDOWS):
        c0 = g * POOL_GROUP_DIM
        cur = uext[POOL_HALO:POOL_HALO + t, c0:c0 + POOL_GROUP_DIM]
        s = cur
        for d in range(1, w):
            s = s + uext[POOL_HALO - d:POOL_HALO - d + t, c0:c0 + POOL_GROUP_DIM]
        inv_count = 1.0 / jnp.minimum(pos + 1, w).astype(F32)
        pooled = s * inv_count - cur
        yp = jnp.dot(pooled.astype(BF16), poolw_ref[g], preferred_element_type=F32)
        ycat[:, c0:c0 + POOL_GROUP_DIM] = (yp * pscale_ref[:, c0:c0 + POOL_GROUP_DIM]).astype(BF16)

    shifted_rows = CONV_HALO + t - SUBLANES
    for r in range(1, SUBLANES):
        vsh[r, 0:shifted_rows, :] = vsh[0, r:r + shifted_rows, :]

    def conv_chunk(c, carry):
        r0 = pl.multiple_of(c * CONV_ROWS, CONV_ROWS)
        acc = jnp.broadcast_to(convb_ref[...], (CONV_ROWS, CONV_WIDTH))
        for k in range(CONV_KERNEL):
            q, r = divmod(CONV_HALO - (CONV_KERNEL - 1) + k, SUBLANES)
            acc = acc + convw_ref[k:k + 1, :] * vsh[r, pl.ds(r0 + q * SUBLANES, CONV_ROWS), :]
        cbuf[pl.ds(r0, CONV_ROWS), :] = acc
        return carry

    lax.fori_loop(0, t // CONV_ROWS, conv_chunk, 0)

    for c in range(CONV_WIDTH // LN_BLOCK):
        c0 = c * LN_BLOCK
        yc = cbuf[:, c0:c0 + LN_BLOCK]
        hi, lo = _split_bf16(yc)
        mu = (jnp.dot(hi, avg_ref[...], preferred_element_type=F32)
              + jnp.dot(lo, avg_ref[...], preferred_element_type=F32))
        dlt = yc - mu
        hi, lo = _split_bf16(dlt * dlt)
        var = (jnp.dot(hi, avg_ref[...], preferred_element_type=F32)
               + jnp.dot(lo, avg_ref[...], preferred_element_type=F32))
        yn = dlt * lax.rsqrt(var + LN_EPS) * lng_ref[:, c0:c0 + LN_BLOCK] + lnb_ref[:, c0:c0 + LN_BLOCK]
        ycat[:, POOL_WIDTH + c0:POOL_WIDTH + c0 + LN_BLOCK] = _silu(yn).astype(BF16)

    y = jnp.dot(ycat[...], wout_ref[...], preferred_element_type=F32)
    x1 = x + gate1 * y
    xo_ref[...] = x1

    ms2 = jnp.mean(x1 * x1, axis=-1, keepdims=True)
    h2 = (x1 * lax.rsqrt(ms2 + RMS_EPS) * n2g_ref[...]) * (1.0 + scale2) + shift2
    for s in range(SUBLANES):
        h2_ref[pl.ds(s, t, stride=SUBLANES), :] = h2[:, s * LANES:(s + 1) * LANES]
    hi, lo = _split_bf16(h2)
    logits = (jnp.dot(hi, wrh_ref[...], preferred_element_type=F32)
              + jnp.dot(lo, wrh_ref[...], preferred_element_type=F32)
              + jnp.dot(hi, wrl_ref[...], preferred_element_type=F32)) + br_ref[...]
    route_ref[...] = _route(logits, tril_ref, cnt_scr)
    cnt_ref[...] = cnt_scr[...]


def _mixer(x, mod, n1g, win, poolw, pscale, convw, convb, lng, lnb, avg, wout, n2g, wrh, wrl, br, tril):
    b, s, d = x.shape
    t = MIX_TILE
    assert d == SUBLANES * LANES
    const = lambda shape: pl.BlockSpec(shape, lambda i, j: (0,) * len(shape))
    tok = lambda width: pl.BlockSpec((None, t, width), lambda i, j: (i, j, 0))
    return pl.pallas_call(
        _mixer_kernel,
        grid=(b, s // t),
        in_specs=[
            tok(d),
            pl.BlockSpec((None, N_MOD, d), lambda i, j: (i, 0, 0)),
            const((1, d)),
            const(win.shape), const(poolw.shape), const((1, POOL_WIDTH)),
            const(convw.shape), const((1, CONV_WIDTH)), const((1, CONV_WIDTH)), const((1, CONV_WIDTH)),
            const(avg.shape), const(wout.shape), const((1, d)),
            const(wrh.shape), const(wrl.shape), const((1, LANES)), const(tril.shape),
        ],
        out_specs=[
            tok(d),
            pl.BlockSpec((None, t * SUBLANES, LANES), lambda i, j: (i, j, 0)),
            tok(LANES),
            pl.BlockSpec((None, 1, LANES), lambda i, j: (i, 0, 0)),
        ],
        out_shape=[
            jax.ShapeDtypeStruct((b, s, d), F32),
            jax.ShapeDtypeStruct((b, s * SUBLANES, LANES), F32),
            jax.ShapeDtypeStruct((b, s, LANES), F32),
            jax.ShapeDtypeStruct((b, 1, LANES), F32),
        ],
        scratch_shapes=[
            pltpu.VMEM((POOL_HALO + t, POOL_WIDTH), F32),
            pltpu.VMEM((SUBLANES, CONV_HALO + t, CONV_WIDTH), F32),
            pltpu.VMEM((t, CONV_WIDTH), F32),
            pltpu.VMEM((t, d), BF16),
            pltpu.VMEM((1, LANES), F32),
        ],
        compiler_params=pltpu.CompilerParams(
            dimension_semantics=("arbitrary", "arbitrary"), vmem_limit_bytes=VMEM_LIMIT),
        name="mixer",
    )(x, mod, n1g, win, poolw, pscale, convw, convb, lng, lnb, avg, wout, n2g, wrh, wrl, br, tril)


def _token_rows(tok):
    return pl.ds(pl.multiple_of(tok * SUBLANES, SUBLANES), SUBLANES)


def _token_chunk(tok0, s, n):
    return pl.ds(tok0 * SUBLANES + s, n, stride=SUBLANES)


def _moe_kernel(pos_ref, wts_ref, tbl_ref, h_ref, wgu_ref, wd_ref, x_ref, mod_ref, fg_ref, o_ref,
                gbuf, stage, *, final_norm, seq):
    nd = seq // MOE_SUB
    ne = N_EXPERTS // EXPERTS_PER_STEP
    b = pl.program_id(0)
    k = pl.program_id(1)
    base = b * (2 * seq)

    @pl.when(k == 0)
    def _():
        gbuf[pl.ds(2 * seq * SUBLANES, EXPERT_TILE * SUBLANES), :] = jnp.zeros((EXPERT_TILE * SUBLANES, LANES), F32)

    @pl.when(k < nd)
    def _():
        t0 = k * MOE_SUB

        def body(i, carry):
            for u in range(ROW_UNROLL):
                t = i * ROW_UNROLL + u
                row = h_ref[_token_rows(t), :]
                gbuf[_token_rows(pos_ref[base + t0 + t]), :] = row
                gbuf[_token_rows(pos_ref[base + seq + t0 + t]), :] = row
            return carry

        lax.fori_loop(0, MOE_SUB // ROW_UNROLL, body, 0)

    @pl.when((k >= nd) & (k < nd + ne))
    def _():
        starts, counts, ntiles = [], [], []
        for j in range(EXPERTS_PER_STEP):
            e = (k - nd) * EXPERTS_PER_STEP + j
            starts.append(tbl_ref[(b * N_EXPERTS + e) * 2])
            counts.append(tbl_ref[(b * N_EXPERTS + e) * 2 + 1])
            ntiles.append((counts[j] + EXPERT_TILE - 1) // EXPERT_TILE)

        def tile(i, carry):
            r0s = [jnp.where(i < ntiles[j], starts[j] + i * EXPERT_TILE, 2 * seq) for j in range(EXPERTS_PER_STEP)]
            xss = [[gbuf[_token_chunk(r0s[j], s, EXPERT_TILE), :] for s in range(SUBLANES)]
                   for j in range(EXPERTS_PER_STEP)]
            acts = [jnp.dot(jnp.concatenate(xss[j], axis=-1).astype(BF16), wgu_ref[j], preferred_element_type=F32)
                    for j in range(EXPERTS_PER_STEP)]
            hids = [(_silu(a[:, :EXPERT_HIDDEN]) * a[:, EXPERT_HIDDEN:]).astype(BF16) for a in acts]
            ys = [jnp.dot(hids[j], wd_ref[j], preferred_element_type=F32) for j in range(EXPERTS_PER_STEP)]
            for j in range(EXPERTS_PER_STEP):
                valid = lax.broadcasted_iota(jnp.int32, (EXPERT_TILE, 1), 0) + i * EXPERT_TILE < counts[j]
                for s in range(SUBLANES):
                    gbuf[_token_chunk(r0s[j], s, EXPERT_TILE), :] = jnp.where(
                        valid, ys[j][:, s * LANES:(s + 1) * LANES], xss[j][s])
            return carry

        lax.fori_loop(0, functools.reduce(jnp.maximum, ntiles), tile, 0)

    @pl.when(k >= nd + ne)
    def _():
        t0 = (k - nd - ne) * MOE_SUB

        def body(i, carry):
            for u in range(ROW_UNROLL):
                t = i * ROW_UNROLL + u
                y1 = gbuf[_token_rows(pos_ref[base + t0 + t]), :]
                y2 = gbuf[_token_rows(pos_ref[base + seq + t0 + t]), :]
                stage[_token_rows(t), :] = wts_ref[base + t0 + t] * y1 + wts_ref[base + seq + t0 + t] * y2
            return carry

        lax.fori_loop(0, MOE_SUB // ROW_UNROLL, body, 0)
        for s in range(SUBLANES):
            sl = slice(s * LANES, (s + 1) * LANES)
            o_ref[:, sl] = x_ref[:, sl] + mod_ref[5:6, sl] * stage[_token_chunk(0, s, MOE_SUB), :]
        if final_norm:
            x2 = o_ref[...]
            ms = jnp.mean(x2 * x2, axis=-1, keepdims=True)
            o_ref[...] = x2 * lax.rsqrt(ms + RMS_EPS) * fg_ref[...]


def _moe(h2r, pos, wts, tbl, wgu, wd, x1, mod, final_g, final_norm):
    b, s, d = x1.shape
    nd = s // MOE_SUB
    ne = N_EXPERTS // EXPERTS_PER_STEP
    clip = lambda v, hi: jnp.minimum(jnp.maximum(v, 0), hi)
    tok_in = lambda i, k, *_: (i, clip(k, nd - 1), 0)
    tok_out = lambda i, k, *_: (i, clip(k - nd - ne, nd - 1), 0)
    expert = lambda i, k, *_: (clip(k - nd, ne - 1), 0, 0)
    grid_spec = pltpu.PrefetchScalarGridSpec(
        num_scalar_prefetch=3,
        grid=(b, nd + ne + nd),
        in_specs=[
            pl.BlockSpec((None, MOE_SUB * SUBLANES, LANES), tok_in),
            pl.BlockSpec((EXPERTS_PER_STEP, d, 2 * EXPERT_HIDDEN), expert),
            pl.BlockSpec((EXPERTS_PER_STEP, EXPERT_HIDDEN, d), expert),
            pl.BlockSpec((None, MOE_SUB, d), tok_out),
            pl.BlockSpec((None, N_MOD, d), lambda i, k, *_: (i, 0, 0)),
            pl.BlockSpec((1, d), lambda i, k, *_: (0, 0)),
        ],
        out_specs=pl.BlockSpec((None, MOE_SUB, d), tok_out),
        scratch_shapes=[
            pltpu.VMEM(((2 * s + EXPERT_TILE) * SUBLANES, LANES), F32),
            pltpu.VMEM((MOE_SUB * SUBLANES, LANES), F32),
        ],
    )
    return pl.pallas_call(
        functools.partial(_moe_kernel, final_norm=final_norm, seq=s),
        grid_spec=grid_spec,
        out_shape=jax.ShapeDtypeStruct((b, s, d), F32),
        compiler_params=pltpu.CompilerParams(
            dimension_semantics=("arbitrary", "arbitrary"), vmem_limit_bytes=MOE_VMEM_LIMIT),
        name="moe",
    )(pos, wts, tbl, h2r, wgu, wd, x1, mod, final_g)


def _dispatch_tables(route, counts):
    b, s, _ = route.shape
    ids = route[..., 0:2].astype(jnp.int32)
    ranks = route[..., 4:6].astype(jnp.int32)
    cnt = counts[:, 0, :N_EXPERTS].astype(jnp.int32)
    starts = jnp.cumsum(cnt, axis=1) - cnt
    hot = ids[..., None] == jnp.arange(N_EXPERTS, dtype=jnp.int32)
    pos = jnp.sum(jnp.where(hot, starts[:, None, None, :], 0), axis=-1) + ranks
    pos = jnp.swapaxes(pos, 1, 2).reshape(-1)
    wts = jnp.swapaxes(route[..., 2:4], 1, 2).reshape(-1)
    tbl = jnp.stack([starts, cnt], axis=-1).reshape(-1)
    return pos, wts, tbl


def _router_weights(rg_w, rg_b, re_w, re_b):
    d = rg_w.shape[0]
    pad = LANES - N_EXPERTS - N_GROUPS
    w = jnp.concatenate([re_w, rg_w, jnp.zeros((d, pad), F32)], axis=1)
    bias = jnp.concatenate([re_b, rg_b, jnp.zeros((pad,), F32)]).reshape(1, LANES)
    w_hi = w.astype(BF16)
    w_lo = (w - w_hi.astype(F32)).astype(BF16)
    return w_hi, w_lo, bias


def kernel(x, c, ada_w, ada_b, norm1_g, w_in, pool_w, pool_scale, conv_w, conv_b, conv_ln_g, conv_ln_b,
           w_out, norm2_g, router_group_w, router_group_b, router_expert_w, router_expert_b,
           expert_w_gate, expert_w_up, expert_w_down, final_g):
    depth = ada_w.shape[0]
    b = x.shape[0]
    mods = _adaln(c, ada_w, ada_b).reshape(depth, b, N_MOD, D_MODEL)
    head = lax.broadcasted_iota(jnp.int32, (LN_BLOCK, LN_BLOCK), 0) // CONV_HEAD_DIM
    head_t = lax.broadcasted_iota(jnp.int32, (LN_BLOCK, LN_BLOCK), 1) // CONV_HEAD_DIM
    avg = jnp.where(head == head_t, 1.0 / CONV_HEAD_DIM, 0.0).astype(BF16)
    earlier = (lax.broadcasted_iota(jnp.int32, (MIX_TILE, MIX_TILE), 1)
               < lax.broadcasted_iota(jnp.int32, (MIX_TILE, MIX_TILE), 0))
    tril = jnp.where(earlier, 1.0, 0.0).astype(BF16)
    row = lambda v: v.reshape(1, -1)
    for l in range(depth):
        wrh, wrl, br = _router_weights(router_group_w[l], router_group_b[l],
                                       router_expert_w[l], router_expert_b[l])
        x1, h2r, route, counts = _mixer(
            x, mods[l], row(norm1_g[l]), w_in[l].astype(BF16), pool_w[l].astype(BF16), row(pool_scale[l]),
            conv_w[l], row(conv_b[l]), row(conv_ln_g[l]), row(conv_ln_b[l]), avg,
            w_out[l].astype(BF16), row(norm2_g[l]), wrh, wrl, br, tril)
        pos, wts, tbl = _dispatch_tables(route, counts)
        wgu = jnp.concatenate([expert_w_gate[l], expert_w_up[l]], axis=-1).astype(BF16)
        x = _moe(h2r, pos, wts, tbl, wgu, expert_w_down[l].astype(BF16), x1, mods[l], row(final_g),
                 final_norm=(l == depth - 1))
    return x
```

```python
import functools

import jax
import jax.numpy as jnp
from jax import lax
from jax.experimental import pallas as pl
from jax.experimental.pallas import tpu as pltpu

D_MODEL = 1024
POOL_WIDTH = 512
POOL_GROUPS = 4
POOL_GROUP_DIM = 128
POOL_WINDOWS = (2, 4, 8, 16)
CONV_WIDTH = 512
CONV_HEAD_DIM = 64
CONV_KERNEL = 31
N_GROUPS = 4
EXPERTS_PER_GROUP = 8
N_EXPERTS = 32
EXPERT_HIDDEN = 256
N_MOD = 6
RMS_EPS = 1e-6
LN_EPS = 1e-5

LANES = 128
SUBLANES = 8
POOL_HALO = SUBLANES * len(POOL_WINDOWS)
CONV_HALO = 32
MIX_TILE = 512
CONV_ROWS = 32
LN_BLOCK = 256
MOE_SUB = 512
EXPERTS_PER_STEP = 2
EXPERT_TILE = 128
ROW_UNROLL = 16
VMEM_LIMIT = 56 * 1024 * 1024
MOE_VMEM_LIMIT = 60 * 1024 * 1024

F32 = jnp.float32
BF16 = jnp.bfloat16


def _silu(v):
    return v * jax.nn.sigmoid(v)


def _split_bf16(v):
    hi = v.astype(BF16)
    lo = (v - hi.astype(F32)).astype(BF16)
    return hi, lo


def _adaln_kernel(c_ref, w_ref, b_ref, o_ref):
    cond = _silu(c_ref[...])
    o_ref[...] = jnp.dot(cond, w_ref[...], preferred_element_type=F32,
                         precision=lax.Precision.HIGHEST) + b_ref[...]


def _adaln(c, ada_w, ada_b):
    depth, d, n = ada_w.shape
    b = c.shape[0]
    tn = D_MODEL
    return pl.pallas_call(
        _adaln_kernel,
        grid=(depth, n // tn),
        in_specs=[
            pl.BlockSpec((b, d), lambda l, j: (0, 0)),
            pl.BlockSpec((None, d, tn), lambda l, j: (l, 0, j)),
            pl.BlockSpec((None, 1, tn), lambda l, j: (l, 0, j)),
        ],
        out_specs=pl.BlockSpec((None, b, tn), lambda l, j: (l, 0, j)),
        out_shape=jax.ShapeDtypeStruct((depth, b, n), F32),
        compiler_params=pltpu.CompilerParams(dimension_semantics=("arbitrary", "arbitrary")),
        name="adaln",
    )(c, ada_w, ada_b.reshape(depth, 1, n))


def _route(logits, tril_ref, cnt_scr):
    t = logits.shape[0]
    lane = lax.broadcasted_iota(jnp.int32, (t, LANES), 1)
    lanef = lane.astype(F32)
    neg = float("-inf")
    big = float(4 * LANES)
    is_group = (lane >= N_EXPERTS) & (lane < N_EXPERTS + N_GROUPS)
    gl = jnp.where(is_group, logits, neg)
    gmax = jnp.max(gl, axis=-1, keepdims=True)
    gidx = jnp.min(jnp.where(gl == gmax, lanef, big), axis=-1, keepdims=True)
    p_group = 1.0 / jnp.sum(jnp.exp(gl - gmax), axis=-1, keepdims=True)
    gsel = gidx.astype(jnp.int32) - N_EXPERTS
    in_group = (lane < N_EXPERTS) & ((lane >> 3) == gsel)
    el = jnp.where(in_group, logits, neg)
    m1 = jnp.max(el, axis=-1, keepdims=True)
    i1 = jnp.min(jnp.where(el == m1, lanef, big), axis=-1, keepdims=True)
    el2 = jnp.where(lanef == i1, neg, el)
    m2 = jnp.max(el2, axis=-1, keepdims=True)
    i2 = jnp.min(jnp.where(el2 == m2, lanef, big), axis=-1, keepdims=True)
    ratio = jnp.exp(m2 - m1)
    w1 = p_group / (1.0 + ratio)
    w2 = p_group * ratio / (1.0 + ratio)
    hot1 = lanef == i1
    hot2 = lanef == i2
    onehot = jnp.where(hot1, 1.0, 0.0) + jnp.where(hot2, 1.0, 0.0)
    before = jnp.dot(tril_ref[...], onehot.astype(BF16), preferred_element_type=F32) + cnt_scr[...]
    rank1 = jnp.sum(jnp.where(hot1, before, 0.0), axis=-1, keepdims=True)
    rank2 = jnp.sum(jnp.where(hot2, before, 0.0), axis=-1, keepdims=True)
    cnt_scr[...] = cnt_scr[...] + jnp.sum(onehot, axis=0, keepdims=True)
    slab = jnp.zeros((t, LANES), F32)
    for col, val in enumerate((i1, i2, w1, w2, rank1, rank2)):
        slab = jnp.where(lane == col, val, slab)
    return slab


def _mixer_kernel(x_ref, mod_ref, n1g_ref, win_ref, poolw_ref, pscale_ref, convw_ref, convb_ref,
                  lng_ref, lnb_ref, avg_ref, wout_ref, n2g_ref, wrh_ref, wrl_ref, br_ref, tril_ref,
                  xo_ref, h2_ref, route_ref, cnt_ref,
                  uext, pbuf, vsh, cbuf, ycat, cnt_scr):
    t = MIX_TILE
    j = pl.program_id(1)

    @pl.when(j == 0)
    def _():
        cnt_scr[...] = jnp.zeros_like(cnt_scr)
        uext[0:POOL_HALO, :] = jnp.zeros((POOL_HALO, POOL_WIDTH), F32)
        vsh[0, 0:CONV_HALO, :] = jnp.zeros((CONV_HALO, CONV_WIDTH), F32)

    @pl.when(j > 0)
    def _():
        uext[0:POOL_HALO, :] = uext[t:t + POOL_HALO, :]
        vsh[0, 0:CONV_HALO, :] = vsh[0, t:t + CONV_HALO, :]

    x = x_ref[...]
    shift1, scale1, gate1 = mod_ref[0:1, :], mod_ref[1:2, :], mod_ref[2:3, :]
    shift2, scale2, gate2 = mod_ref[3:4, :], mod_ref[4:5, :], mod_ref[5:6, :]
    del gate2

    ms = jnp.mean(x * x, axis=-1, keepdims=True)
    h = x * lax.rsqrt(ms + RMS_EPS) * (n1g_ref[...] * (1.0 + scale1)) + shift1
    hb = h.astype(BF16)

    uext[POOL_HALO:POOL_HALO + t, :] = jnp.dot(hb, win_ref[:, 0:POOL_WIDTH], preferred_element_type=F32)
    glu_a = jnp.dot(hb, win_ref[:, POOL_WIDTH:POOL_WIDTH + CONV_WIDTH], preferred_element_type=F32)
    glu_g = jnp.dot(hb, win_ref[:, POOL_WIDTH + CONV_WIDTH:], preferred_element_type=F32)
    vsh[0, CONV_HALO:CONV_HALO + t, :] = glu_a * jax.nn.sigmoid(glu_g)

    pos = j * t + lax.broadcasted_iota(jnp.int32, (t, 1), 0)
    rows = POOL_HALO + t
    for g, w in enumerate(POOL_WINDOWS):
        assert w == 2 << g
        half, lo, c0 = w // 2, SUBLANES * (g + 1), g * POOL_GROUP_DIM
        if g == 0:
            level = uext[lo:rows, c0:] + uext[lo - half:rows - half, c0:]
        else:
            level = pbuf[g - 1, lo:rows, c0:] + pbuf[g - 1, lo - half:rows - half, c0:]
        if g + 1 < len(POOL_WINDOWS):
            pbuf[g, lo:rows, c0:] = level
        s = level[POOL_HALO - lo:, 0:POOL_GROUP_DIM]
        cur = uext[POOL_HALO:rows, c0:c0 + POOL_GROUP_DIM]
        inv_count = 1.0 / jnp.minimum(pos + 1, w).astype(F32)
        pooled = s * inv_count - cur
        yp = jnp.dot(pooled.astype(BF16), poolw_ref[g], preferred_element_type=F32)
        ycat[:, c0:c0 + POOL_GROUP_DIM] = (yp * pscale_ref[:, c0:c0 + POOL_GROUP_DIM]).astype(BF16)

    shifted_rows = CONV_HALO + t - SUBLANES
    for r in range(1, SUBLANES):
        vsh[r, 0:shifted_rows, :] = vsh[0, r:r + shifted_rows, :]

    def conv_chunk(c, carry):
        r0 = pl.multiple_of(c * CONV_ROWS, CONV_ROWS)
        acc = jnp.broadcast_to(convb_ref[...], (CONV_ROWS, CONV_WIDTH))
        for k in range(CONV_KERNEL):
            q, r = divmod(CONV_HALO - (CONV_KERNEL - 1) + k, SUBLANES)
            acc = acc + convw_ref[k:k + 1, :] * vsh[r, pl.ds(r0 + q * SUBLANES, CONV_ROWS), :]
        cbuf[pl.ds(r0, CONV_ROWS), :] = acc
        return carry

    lax.fori_loop(0, t // CONV_ROWS, conv_chunk, 0)

    for c in range(CONV_WIDTH // LN_BLOCK):
        c0 = c * LN_BLOCK
        yc = cbuf[:, c0:c0 + LN_BLOCK]
        hi, lo = _split_bf16(yc)
        mu = (jnp.dot(hi, avg_ref[...], preferred_element_type=F32)
              + jnp.dot(lo, avg_ref[...], preferred_element_type=F32))
        dlt = yc - mu
        hi, lo = _split_bf16(dlt * dlt)
        var = (jnp.dot(hi, avg_ref[...], preferred_element_type=F32)
               + jnp.dot(lo, avg_ref[...], preferred_element_type=F32))
        yn = dlt * lax.rsqrt(var + LN_EPS) * lng_ref[:, c0:c0 + LN_BLOCK] + lnb_ref[:, c0:c0 + LN_BLOCK]
        ycat[:, POOL_WIDTH + c0:POOL_WIDTH + c0 + LN_BLOCK] = _silu(yn).astype(BF16)

    y = jnp.dot(ycat[...], wout_ref[...], preferred_element_type=F32)
    x1 = x + gate1 * y
    xo_ref[...] = x1

    ms2 = jnp.mean(x1 * x1, axis=-1, keepdims=True)
    h2 = x1 * lax.rsqrt(ms2 + RMS_EPS) * (n2g_ref[...] * (1.0 + scale2)) + shift2
    for s in range(SUBLANES):
        h2_ref[pl.ds(s, t, stride=SUBLANES), :] = h2[:, s * LANES:(s + 1) * LANES]
    hi, lo = _split_bf16(h2)
    logits = (jnp.dot(hi, wrh_ref[...], preferred_element_type=F32)
              + jnp.dot(lo, wrh_ref[...], preferred_element_type=F32)
              + jnp.dot(hi, wrl_ref[...], preferred_element_type=F32)) + br_ref[...]
    route_ref[...] = _route(logits, tril_ref, cnt_scr).T[0:SUBLANES, :]
    cnt_ref[...] = cnt_scr[...]


def _mixer(x, mod, n1g, win, poolw, pscale, convw, convb, lng, lnb, avg, wout, n2g, wrh, wrl, br, tril):
    b, s, d = x.shape
    t = MIX_TILE
    assert d == SUBLANES * LANES
    const = lambda shape: pl.BlockSpec(shape, lambda i, j: (0,) * len(shape))
    tok = lambda width: pl.BlockSpec((None, t, width), lambda i, j: (i, j, 0))
    return pl.pallas_call(
        _mixer_kernel,
        grid=(b, s // t),
        in_specs=[
            tok(d),
            pl.BlockSpec((None, N_MOD, d), lambda i, j: (i, 0, 0)),
            const((1, d)),
            const(win.shape), const(poolw.shape), const((1, POOL_WIDTH)),
            const(convw.shape), const((1, CONV_WIDTH)), const((1, CONV_WIDTH)), const((1, CONV_WIDTH)),
            const(avg.shape), const(wout.shape), const((1, d)),
            const(wrh.shape), const(wrl.shape), const((1, LANES)), const(tril.shape),
        ],
        out_specs=[
            tok(d),
            pl.BlockSpec((None, t * SUBLANES, LANES), lambda i, j: (i, j, 0)),
            pl.BlockSpec((None, SUBLANES, t), lambda i, j: (i, 0, j)),
            pl.BlockSpec((None, 1, LANES), lambda i, j: (i, 0, 0)),
        ],
        out_shape=[
            jax.ShapeDtypeStruct((b, s, d), F32),
            jax.ShapeDtypeStruct((b, s * SUBLANES, LANES), F32),
            jax.ShapeDtypeStruct((b, SUBLANES, s), F32),
            jax.ShapeDtypeStruct((b, 1, LANES), F32),
        ],
        scratch_shapes=[
            pltpu.VMEM((POOL_HALO + t, POOL_WIDTH), F32),
            pltpu.VMEM((len(POOL_WINDOWS) - 1, POOL_HALO + t, POOL_WIDTH), F32),
            pltpu.VMEM((SUBLANES, CONV_HALO + t, CONV_WIDTH), F32),
            pltpu.VMEM((t, CONV_WIDTH), F32),
            pltpu.VMEM((t, d), BF16),
            pltpu.VMEM((1, LANES), F32),
        ],
        compiler_params=pltpu.CompilerParams(
            dimension_semantics=("arbitrary", "arbitrary"), vmem_limit_bytes=VMEM_LIMIT),
        name="mixer",
    )(x, mod, n1g, win, poolw, pscale, convw, convb, lng, lnb, avg, wout, n2g, wrh, wrl, br, tril)


def _token_rows(row0):
    return pl.ds(pl.multiple_of(row0, SUBLANES), SUBLANES)


def _token_chunk(tok0, s, n):
    return pl.ds(tok0 * SUBLANES + s, n, stride=SUBLANES)


def _moe_kernel(pos_ref, wts_ref, tbl_ref, h_ref, wgu_ref, wd_ref, x_ref, mod_ref, fg_ref, o_ref,
                gbuf, stage, *, final_norm, seq):
    nd = seq // MOE_SUB
    ne = N_EXPERTS // EXPERTS_PER_STEP
    b = pl.program_id(0)
    k = pl.program_id(1)
    base = b * (2 * seq)

    @pl.when(k == 0)
    def _():
        gbuf[pl.ds(2 * seq * SUBLANES, EXPERT_TILE * SUBLANES), :] = jnp.zeros((EXPERT_TILE * SUBLANES, LANES), F32)

    @pl.when(k < nd)
    def _():
        t0 = k * MOE_SUB

        def body(i, carry):
            for u in range(ROW_UNROLL):
                t = i * ROW_UNROLL + u
                row = h_ref[_token_rows(t * SUBLANES), :]
                gbuf[_token_rows(pos_ref[base + t0 + t]), :] = row
                gbuf[_token_rows(pos_ref[base + seq + t0 + t]), :] = row
            return carry

        lax.fori_loop(0, MOE_SUB // ROW_UNROLL, body, 0)

    @pl.when((k >= nd) & (k < nd + ne))
    def _():
        starts, counts, ntiles = [], [], []
        for j in range(EXPERTS_PER_STEP):
            e = (k - nd) * EXPERTS_PER_STEP + j
            starts.append(tbl_ref[(b * N_EXPERTS + e) * 2])
            counts.append(tbl_ref[(b * N_EXPERTS + e) * 2 + 1])
            ntiles.append((counts[j] + EXPERT_TILE - 1) // EXPERT_TILE)

        def tile(i, carry):
            r0s = [jnp.where(i < ntiles[j], starts[j] + i * EXPERT_TILE, 2 * seq) for j in range(EXPERTS_PER_STEP)]
            xss = [[gbuf[_token_chunk(r0s[j], s, EXPERT_TILE), :] for s in range(SUBLANES)]
                   for j in range(EXPERTS_PER_STEP)]
            acts = [jnp.dot(jnp.concatenate(xss[j], axis=-1).astype(BF16), wgu_ref[j], preferred_element_type=F32)
                    for j in range(EXPERTS_PER_STEP)]
            hids = [(_silu(a[:, :EXPERT_HIDDEN]) * a[:, EXPERT_HIDDEN:]).astype(BF16) for a in acts]
            ys = [jnp.dot(hids[j], wd_ref[j], preferred_element_type=F32) for j in range(EXPERTS_PER_STEP)]
            for j in range(EXPERTS_PER_STEP):
                valid = lax.broadcasted_iota(jnp.int32, (EXPERT_TILE, 1), 0) + i * EXPERT_TILE < counts[j]
                for s in range(SUBLANES):
                    gbuf[_token_chunk(r0s[j], s, EXPERT_TILE), :] = jnp.where(
                        valid, ys[j][:, s * LANES:(s + 1) * LANES], xss[j][s])
            return carry

        lax.fori_loop(0, functools.reduce(jnp.maximum, ntiles), tile, 0)

    @pl.when(k >= nd + ne)
    def _():
        t0 = (k - nd - ne) * MOE_SUB

        def body(i, carry):
            for u in range(ROW_UNROLL):
                t = i * ROW_UNROLL + u
                y1 = gbuf[_token_rows(pos_ref[base + t0 + t]), :]
                y2 = gbuf[_token_rows(pos_ref[base + seq + t0 + t]), :]
                stage[_token_rows(t * SUBLANES), :] = wts_ref[base + t0 + t] * y1 + wts_ref[base + seq + t0 + t] * y2
            return carry

        lax.fori_loop(0, MOE_SUB // ROW_UNROLL, body, 0)
        for s in range(SUBLANES):
            sl = slice(s * LANES, (s + 1) * LANES)
            o_ref[:, sl] = x_ref[:, sl] + mod_ref[5:6, sl] * stage[_token_chunk(0, s, MOE_SUB), :]
        if final_norm:
            x2 = o_ref[...]
            ms = jnp.mean(x2 * x2, axis=-1, keepdims=True)
            o_ref[...] = x2 * lax.rsqrt(ms + RMS_EPS) * fg_ref[...]


def _moe(h2r, pos, wts, tbl, wgu, wd, x1, mod, final_g, final_norm):
    b, s, d = x1.shape
    nd = s // MOE_SUB
    ne = N_EXPERTS // EXPERTS_PER_STEP
    clip = lambda v, hi: jnp.minimum(jnp.maximum(v, 0), hi)
    tok_in = lambda i, k, *_: (i, clip(k, nd - 1), 0)
    tok_out = lambda i, k, *_: (i, clip(k - nd - ne, nd - 1), 0)
    expert = lambda i, k, *_: (clip(k - nd, ne - 1), 0, 0)
    grid_spec = pltpu.PrefetchScalarGridSpec(
        num_scalar_prefetch=3,
        grid=(b, nd + ne + nd),
        in_specs=[
            pl.BlockSpec((None, MOE_SUB * SUBLANES, LANES), tok_in),
            pl.BlockSpec((EXPERTS_PER_STEP, d, 2 * EXPERT_HIDDEN), expert),
            pl.BlockSpec((EXPERTS_PER_STEP, EXPERT_HIDDEN, d), expert),
            pl.BlockSpec((None, MOE_SUB, d), tok_out),
            pl.BlockSpec((None, N_MOD, d), lambda i, k, *_: (i, 0, 0)),
            pl.BlockSpec((1, d), lambda i, k, *_: (0, 0)),
        ],
        out_specs=pl.BlockSpec((None, MOE_SUB, d), tok_out),
        scratch_shapes=[
            pltpu.VMEM(((2 * s + EXPERT_TILE) * SUBLANES, LANES), F32),
            pltpu.VMEM((MOE_SUB * SUBLANES, LANES), F32),
        ],
    )
    return pl.pallas_call(
        functools.partial(_moe_kernel, final_norm=final_norm, seq=s),
        grid_spec=grid_spec,
        out_shape=jax.ShapeDtypeStruct((b, s, d), F32),
        compiler_params=pltpu.CompilerParams(
            dimension_semantics=("arbitrary", "arbitrary"), vmem_limit_bytes=MOE_VMEM_LIMIT),
        name="moe",
    )(pos, wts, tbl, h2r, wgu, wd, x1, mod, final_g)


def _dispatch_tables(route, counts):
    ids = route[:, 0:2, :].astype(jnp.int32)
    ranks = route[:, 4:6, :].astype(jnp.int32)
    cnt = counts[:, 0, :N_EXPERTS].astype(jnp.int32)
    starts = jnp.cumsum(cnt, axis=1) - cnt
    hot = ids[..., None] == jnp.arange(N_EXPERTS, dtype=jnp.int32)
    pos = jnp.sum(jnp.where(hot, starts[:, None, None, :], 0), axis=-1) + ranks
    row0 = (pos * SUBLANES).reshape(-1)
    wts = route[:, 2:4, :].reshape(-1)
    tbl = jnp.stack([starts, cnt], axis=-1).reshape(-1)
    return row0, wts, tbl


def _router_weights(rg_w, rg_b, re_w, re_b):
    d = rg_w.shape[0]
    pad = LANES - N_EXPERTS - N_GROUPS
    w = jnp.concatenate([re_w, rg_w, jnp.zeros((d, pad), F32)], axis=1)
    bias = jnp.concatenate([re_b, rg_b, jnp.zeros((pad,), F32)]).reshape(1, LANES)
    w_hi = w.astype(BF16)
    w_lo = (w - w_hi.astype(F32)).astype(BF16)
    return w_hi, w_lo, bias


def kernel(x, c, ada_w, ada_b, norm1_g, w_in, pool_w, pool_scale, conv_w, conv_b, conv_ln_g, conv_ln_b,
           w_out, norm2_g, router_group_w, router_group_b, router_expert_w, router_expert_b,
           expert_w_gate, expert_w_up, expert_w_down, final_g):
    depth = ada_w.shape[0]
    b = x.shape[0]
    mods = _adaln(c, ada_w, ada_b).reshape(depth, b, N_MOD, D_MODEL)
    head = lax.broadcasted_iota(jnp.int32, (LN_BLOCK, LN_BLOCK), 0) // CONV_HEAD_DIM
    head_t = lax.broadcasted_iota(jnp.int32, (LN_BLOCK, LN_BLOCK), 1) // CONV_HEAD_DIM
    avg = jnp.where(head == head_t, 1.0 / CONV_HEAD_DIM, 0.0).astype(BF16)
    earlier = (lax.broadcasted_iota(jnp.int32, (MIX_TILE, MIX_TILE), 1)
               < lax.broadcasted_iota(jnp.int32, (MIX_TILE, MIX_TILE), 0))
    tril = jnp.where(earlier, 1.0, 0.0).astype(BF16)
    row = lambda v: v.reshape(1, -1)
    for l in range(depth):
        wrh, wrl, br = _router_weights(router_group_w[l], router_group_b[l],
                                       router_expert_w[l], router_expert_b[l])
        x1, h2r, route, counts = _mixer(
            x, mods[l], row(norm1_g[l]), w_in[l].astype(BF16), pool_w[l].astype(BF16), row(pool_scale[l]),
            conv_w[l], row(conv_b[l]), row(conv_ln_g[l]), row(conv_ln_b[l]), avg,
            w_out[l].astype(BF16), row(norm2_g[l]), wrh, wrl, br, tril)
        pos, wts, tbl = _dispatch_tables(route, counts)
        wgu = jnp.concatenate([expert_w_gate[l], expert_w_up[l]], axis=-1).astype(BF16)
        x = _moe(h2r, pos, wts, tbl, wgu, expert_w_down[l].astype(BF16), x1, mods[l], row(final_g),
                 final_norm=(l == depth - 1))
    return x
```

```python
import functools

import jax
import jax.numpy as jnp
from jax import lax
from jax.experimental import pallas as pl
from jax.experimental.pallas import tpu as pltpu

D_MODEL = 1024
POOL_WIDTH = 512
POOL_GROUPS = 4
POOL_GROUP_DIM = 128
POOL_WINDOWS = (2, 4, 8, 16)
CONV_WIDTH = 512
CONV_HEAD_DIM = 64
CONV_KERNEL = 31
N_GROUPS = 4
EXPERTS_PER_GROUP = 8
N_EXPERTS = 32
EXPERT_HIDDEN = 256
N_MOD = 6
RMS_EPS = 1e-6
LN_EPS = 1e-5

LANES = 128
SUBLANES = 8
POOL_HALO = SUBLANES * len(POOL_WINDOWS)
CONV_HALO = 32
MIX_TILE = 512
CONV_ROWS = 32
CONV_STRIDE = 4
LN_BLOCK = 256
TAIL_ROWS = 128
PROJ_PIECE = 384
MOE_SUB = 512
EXPERTS_PER_STEP = 2
EXPERT_TILE = 128
ROW_UNROLL = 16
VMEM_LIMIT = 56 * 1024 * 1024
MOE_VMEM_LIMIT = 60 * 1024 * 1024

F32 = jnp.float32
BF16 = jnp.bfloat16


def _silu(v):
    return v * jax.nn.sigmoid(v)


def _split_bf16(v):
    hi = v.astype(BF16)
    lo = (v - hi.astype(F32)).astype(BF16)
    return hi, lo


def _adaln_kernel(c_ref, w_ref, b_ref, o_ref):
    cond = _silu(c_ref[...])
    o_ref[...] = jnp.dot(cond, w_ref[...], preferred_element_type=F32,
                         precision=lax.Precision.HIGHEST) + b_ref[...]


def _adaln(c, ada_w, ada_b):
    depth, d, n = ada_w.shape
    b = c.shape[0]
    tn = D_MODEL
    return pl.pallas_call(
        _adaln_kernel,
        grid=(depth, n // tn),
        in_specs=[
            pl.BlockSpec((b, d), lambda l, j: (0, 0)),
            pl.BlockSpec((None, d, tn), lambda l, j: (l, 0, j)),
            pl.BlockSpec((None, 1, tn), lambda l, j: (l, 0, j)),
        ],
        out_specs=pl.BlockSpec((None, b, tn), lambda l, j: (l, 0, j)),
        out_shape=jax.ShapeDtypeStruct((depth, b, n), F32),
        compiler_params=pltpu.CompilerParams(dimension_semantics=("arbitrary", "arbitrary")),
        name="adaln",
    )(c, ada_w, ada_b.reshape(depth, 1, n))


def _route(logits, tril_ref, cnt_scr):
    t = logits.shape[0]
    lane = lax.broadcasted_iota(jnp.int32, (t, LANES), 1)
    lanef = lane.astype(F32)
    neg = float("-inf")
    big = float(4 * LANES)
    is_group = (lane >= N_EXPERTS) & (lane < N_EXPERTS + N_GROUPS)
    gl = jnp.where(is_group, logits, neg)
    gmax = jnp.max(gl, axis=-1, keepdims=True)
    gidx = jnp.min(jnp.where(gl == gmax, lanef, big), axis=-1, keepdims=True)
    p_group = 1.0 / jnp.sum(jnp.exp(gl - gmax), axis=-1, keepdims=True)
    gsel = gidx.astype(jnp.int32) - N_EXPERTS
    in_group = (lane < N_EXPERTS) & ((lane >> 3) == gsel)
    el = jnp.where(in_group, logits, neg)
    m1 = jnp.max(el, axis=-1, keepdims=True)
    i1 = jnp.min(jnp.where(el == m1, lanef, big), axis=-1, keepdims=True)
    el2 = jnp.where(lanef == i1, neg, el)
    m2 = jnp.max(el2, axis=-1, keepdims=True)
    i2 = jnp.min(jnp.where(el2 == m2, lanef, big), axis=-1, keepdims=True)
    ratio = jnp.exp(m2 - m1)
    w1 = p_group / (1.0 + ratio)
    w2 = p_group * ratio / (1.0 + ratio)
    hot1 = lanef == i1
    hot2 = lanef == i2
    onehot = jnp.where(hot1, 1.0, 0.0) + jnp.where(hot2, 1.0, 0.0)
    before = jnp.dot(tril_ref[...], onehot.astype(BF16), preferred_element_type=F32) + cnt_scr[...]
    rank1 = jnp.sum(jnp.where(hot1, before, 0.0), axis=-1, keepdims=True)
    rank2 = jnp.sum(jnp.where(hot2, before, 0.0), axis=-1, keepdims=True)
    cnt_scr[...] = cnt_scr[...] + jnp.sum(onehot, axis=0, keepdims=True)
    slab = jnp.zeros((t, LANES), F32)
    for col, val in enumerate((i1, i2, w1, w2, rank1, rank2)):
        slab = jnp.where(lane == col, val, slab)
    return slab


def _mixer_kernel(x_ref, xnext_ref, mod_ref, modnext_ref, n1g_ref, win_ref, poolw_ref, pscale_ref, convw_ref,
                  convb_ref, lng_ref, lnb_ref, avg_ref, wout_ref, n2g_ref, wr_ref, br_ref, tril_ref,
                  xo_ref, h2_ref, route_ref, cnt_ref,
                  uext, pbuf, vbuf, cbuf, ycat, hbuf, zbuf, cnt_scr):
    t = MIX_TILE
    i = pl.program_id(0)
    j = pl.program_id(1)
    gate1 = mod_ref[2:3, :]
    shift2, scale2 = mod_ref[3:4, :], mod_ref[4:5, :]

    def norm_in(src_ref, m_ref):
        xin = src_ref[...]
        ms = jnp.mean(xin * xin, axis=-1, keepdims=True)
        gain = n1g_ref[...] * (1.0 + m_ref[1:2, :])
        hbuf[...] = (xin * lax.rsqrt(ms + RMS_EPS) * gain + m_ref[0:1, :]).astype(BF16)

    def project(piece):
        zbuf[piece] = jnp.dot(hbuf[...], win_ref[piece], preferred_element_type=F32)

    def projected(col):
        piece, off = divmod(col, PROJ_PIECE)
        return zbuf[piece, :, off:off + LANES]

    n_pieces = win_ref.shape[0]

    @pl.when((i == 0) & (j == 0))
    def _():
        norm_in(x_ref, mod_ref)
        for piece in range(n_pieces):
            project(piece)

    @pl.when(j == 0)
    def _():
        cnt_scr[...] = jnp.zeros_like(cnt_scr)
        uext[0:POOL_HALO, :] = jnp.zeros((POOL_HALO, POOL_WIDTH), F32)
        vbuf[:, 0:CONV_HALO, :] = jnp.zeros((CONV_WIDTH // LANES, CONV_HALO, LANES), F32)

    @pl.when(j > 0)
    def _():
        uext[0:POOL_HALO, :] = uext[t:t + POOL_HALO, :]
        vbuf[:, 0:CONV_HALO, :] = vbuf[:, t:t + CONV_HALO, :]

    for p in range(POOL_WIDTH // LANES):
        uext[POOL_HALO:POOL_HALO + t, p * LANES:(p + 1) * LANES] = projected(p * LANES)
    for p in range(CONV_WIDTH // LANES):
        value = projected(POOL_WIDTH + p * LANES)
        gate = projected(POOL_WIDTH + CONV_WIDTH + p * LANES)
        vbuf[p, CONV_HALO:CONV_HALO + t, :] = value * jax.nn.sigmoid(gate)

    pos = j * t + lax.broadcasted_iota(jnp.int32, (t, 1), 0)
    rows = POOL_HALO + t
    for g, w in enumerate(POOL_WINDOWS):
        assert w == 2 << g
        half, lo, c0 = w // 2, SUBLANES * (g + 1), g * POOL_GROUP_DIM
        if g == 0:
            level = uext[lo:rows, c0:] + uext[lo - half:rows - half, c0:]
        else:
            level = pbuf[g - 1, lo:rows, c0:] + pbuf[g - 1, lo - half:rows - half, c0:]
        if g + 1 < len(POOL_WINDOWS):
            pbuf[g, lo:rows, c0:] = level
        s = level[POOL_HALO - lo:, 0:POOL_GROUP_DIM]
        cur = uext[POOL_HALO:rows, c0:c0 + POOL_GROUP_DIM]
        inv_count = 1.0 / jnp.minimum(pos + 1, w).astype(F32)
        pooled = s * inv_count - cur
        yp = jnp.dot(pooled.astype(BF16), poolw_ref[g], preferred_element_type=F32)
        ycat[:, c0:c0 + POOL_GROUP_DIM] = (yp * pscale_ref[:, c0:c0 + POOL_GROUP_DIM]).astype(BF16)

    span = CONV_ROWS * CONV_STRIDE

    def conv_block(block):
        r0 = block * span
        for p in range(CONV_WIDTH // LANES):
            lanes = slice(p * LANES, (p + 1) * LANES)
            accs = [jnp.broadcast_to(convb_ref[:, lanes], (CONV_ROWS, LANES))] * CONV_STRIDE
            for s in range(CONV_KERNEL + CONV_STRIDE - 1):
                start = r0 + CONV_HALO - (CONV_KERNEL - 1) + s
                rows = vbuf[p, pl.ds(start, CONV_ROWS, stride=CONV_STRIDE), :]
                for phase in range(CONV_STRIDE):
                    k = s - phase
                    if 0 <= k < CONV_KERNEL:
                        accs[phase] = accs[phase] + convw_ref[k:k + 1, lanes] * rows
            for phase in range(CONV_STRIDE):
                cbuf[p, pl.ds(r0 + phase, CONV_ROWS, stride=CONV_STRIDE), :] = accs[phase]

    norm_in(xnext_ref, modnext_ref)
    assert n_pieces * span == t

    def conv_and_project(piece, carry):
        project(piece)
        conv_block(piece)
        return carry

    lax.fori_loop(0, n_pieces, conv_and_project, 0)

    blocks = [slice(r, r + TAIL_ROWS) for r in range(0, t, TAIL_ROWS)]

    planes_per_block = LN_BLOCK // LANES
    for rs in blocks:
        for c in range(CONV_WIDTH // LN_BLOCK):
            c0 = c * LN_BLOCK
            yc = jnp.concatenate([cbuf[c * planes_per_block + p, rs, :] for p in range(planes_per_block)], axis=-1)
            hi, lo = _split_bf16(yc)
            mu = (jnp.dot(hi, avg_ref[...], preferred_element_type=F32)
                  + jnp.dot(lo, avg_ref[...], preferred_element_type=F32))
            dlt = yc - mu
            hi, lo = _split_bf16(dlt * dlt)
            var = (jnp.dot(hi, avg_ref[...], preferred_element_type=F32)
                   + jnp.dot(lo, avg_ref[...], preferred_element_type=F32))
            yn = dlt * lax.rsqrt(var + LN_EPS) * lng_ref[:, c0:c0 + LN_BLOCK] + lnb_ref[:, c0:c0 + LN_BLOCK]
            ycat[rs, POOL_WIDTH + c0:POOL_WIDTH + c0 + LN_BLOCK] = _silu(yn).astype(BF16)

    x1s = []
    for rs in blocks:
        x1 = x_ref[rs, :] + gate1 * jnp.dot(ycat[rs, :], wout_ref[...], preferred_element_type=F32)
        xo_ref[rs, :] = x1
        x1s.append(x1)

    h2s = []
    for rs, x1 in zip(blocks, x1s):
        ms2 = jnp.mean(x1 * x1, axis=-1, keepdims=True)
        h2 = x1 * lax.rsqrt(ms2 + RMS_EPS) * (n2g_ref[...] * (1.0 + scale2)) + shift2
        for s in range(SUBLANES):
            h2_ref[pl.ds(rs.start * SUBLANES + s, TAIL_ROWS, stride=SUBLANES), :] = h2[:, s * LANES:(s + 1) * LANES]
        h2s.append(h2)

    logits = []
    for h2 in h2s:
        hi, lo = _split_bf16(h2)
        prod = (jnp.dot(hi, wr_ref[...], preferred_element_type=F32)
                + jnp.dot(lo, wr_ref[...], preferred_element_type=F32))
        logits.append(prod[:, 0:LANES] + prod[:, LANES:2 * LANES] + br_ref[...])

    for rs, lg in zip(blocks, logits):
        route_ref[:, rs] = _route(lg, tril_ref, cnt_scr).T[0:SUBLANES, :]
    cnt_ref[...] = cnt_scr[...]


def _mixer(x, mod, n1g, win, poolw, pscale, convw, convb, lng, lnb, avg, wout, n2g, wr, br, tril):
    b, s, d = x.shape
    t = MIX_TILE
    assert d == SUBLANES * LANES and win.shape[1:] == (d, PROJ_PIECE)
    nj = s // t
    succ = lambda i, j: (jnp.minimum(i + (j + 1) // nj, b - 1), jnp.where(i + (j + 1) // nj < b, (j + 1) % nj, j))
    const = lambda shape: pl.BlockSpec(shape, lambda i, j: (0,) * len(shape))
    tok = lambda width: pl.BlockSpec((None, t, width), lambda i, j: (i, j, 0))
    return pl.pallas_call(
        _mixer_kernel,
        grid=(b, s // t),
        in_specs=[
            tok(d),
            pl.BlockSpec((None, t, d), lambda i, j: (*succ(i, j), 0)),
            pl.BlockSpec((None, N_MOD, d), lambda i, j: (i, 0, 0)),
            pl.BlockSpec((None, N_MOD, d), lambda i, j: (succ(i, j)[0], 0, 0)),
            const((1, d)),
            const(win.shape), const(poolw.shape), const((1, POOL_WIDTH)),
            const(convw.shape), const((1, CONV_WIDTH)), const((1, CONV_WIDTH)), const((1, CONV_WIDTH)),
            const(avg.shape), const(wout.shape), const((1, d)),
            const(wr.shape), const((1, LANES)), const(tril.shape),
        ],
        out_specs=[
            tok(d),
            pl.BlockSpec((None, t * SUBLANES, LANES), lambda i, j: (i, j, 0)),
            pl.BlockSpec((None, SUBLANES, t), lambda i, j: (i, 0, j)),
            pl.BlockSpec((None, 1, LANES), lambda i, j: (i, 0, 0)),
        ],
        out_shape=[
            jax.ShapeDtypeStruct((b, s, d), F32),
            jax.ShapeDtypeStruct((b, s * SUBLANES, LANES), F32),
            jax.ShapeDtypeStruct((b, SUBLANES, s), F32),
            jax.ShapeDtypeStruct((b, 1, LANES), F32),
        ],
        scratch_shapes=[
            pltpu.VMEM((POOL_HALO + t, POOL_WIDTH), F32),
            pltpu.VMEM((len(POOL_WINDOWS) - 1, POOL_HALO + t, POOL_WIDTH), F32),
            pltpu.VMEM((CONV_WIDTH // LANES, CONV_HALO + t, LANES), F32),
            pltpu.VMEM((CONV_WIDTH // LANES, t, LANES), F32),
            pltpu.VMEM((t, d), BF16),
            pltpu.VMEM((t, d), BF16),
            pltpu.VMEM((win.shape[0], t, PROJ_PIECE), F32),
            pltpu.VMEM((1, LANES), F32),
        ],
        compiler_params=pltpu.CompilerParams(
            dimension_semantics=("arbitrary", "arbitrary"), vmem_limit_bytes=VMEM_LIMIT),
        name="mixer",
    )(x, x, mod, mod, n1g, win, poolw, pscale, convw, convb, lng, lnb, avg, wout, n2g, wr, br, tril)


def _token_rows(row0):
    return pl.ds(pl.multiple_of(row0, SUBLANES), SUBLANES)


def _token_chunk(tok0, s, n):
    return pl.ds(tok0 * SUBLANES + s, n, stride=SUBLANES)


def _moe_kernel(pos_ref, wts_ref, tbl_ref, h_ref, wgu_ref, wd_ref, x_ref, mod_ref, fg_ref, o_ref,
                gbuf, stage, *, final_norm, seq):
    nd = seq // MOE_SUB
    ne = N_EXPERTS // EXPERTS_PER_STEP
    b = pl.program_id(0)
    k = pl.program_id(1)
    base = b * (2 * seq)

    @pl.when(k == 0)
    def _():
        gbuf[pl.ds(2 * seq * SUBLANES, EXPERT_TILE * SUBLANES), :] = jnp.zeros((EXPERT_TILE * SUBLANES, LANES), F32)

    @pl.when(k < nd)
    def _():
        t0 = k * MOE_SUB

        def body(i, carry):
            for u in range(ROW_UNROLL):
                t = i * ROW_UNROLL + u
                row = h_ref[_token_rows(t * SUBLANES), :]
                gbuf[_token_rows(pos_ref[base + t0 + t]), :] = row
                gbuf[_token_rows(pos_ref[base + seq + t0 + t]), :] = row
            return carry

        lax.fori_loop(0, MOE_SUB // ROW_UNROLL, body, 0)

    @pl.when((k >= nd) & (k < nd + ne))
    def _():
        starts, counts, ntiles = [], [], []
        for j in range(EXPERTS_PER_STEP):
            e = (k - nd) * EXPERTS_PER_STEP + j
            starts.append(tbl_ref[(b * N_EXPERTS + e) * 2])
            counts.append(tbl_ref[(b * N_EXPERTS + e) * 2 + 1])
            ntiles.append((counts[j] + EXPERT_TILE - 1) // EXPERT_TILE)

        def tile(i, carry):
            r0s = [jnp.where(i < ntiles[j], starts[j] + i * EXPERT_TILE, 2 * seq) for j in range(EXPERTS_PER_STEP)]
            xss = [[gbuf[_token_chunk(r0s[j], s, EXPERT_TILE), :] for s in range(SUBLANES)]
                   for j in range(EXPERTS_PER_STEP)]
            acts = [jnp.dot(jnp.concatenate(xss[j], axis=-1).astype(BF16), wgu_ref[j], preferred_element_type=F32)
                    for j in range(EXPERTS_PER_STEP)]
            hids = [(_silu(a[:, :EXPERT_HIDDEN]) * a[:, EXPERT_HIDDEN:]).astype(BF16) for a in acts]
            ys = [jnp.dot(hids[j], wd_ref[j], preferred_element_type=F32) for j in range(EXPERTS_PER_STEP)]
            for j in range(EXPERTS_PER_STEP):
                valid = lax.broadcasted_iota(jnp.int32, (EXPERT_TILE, 1), 0) + i * EXPERT_TILE < counts[j]
                for s in range(SUBLANES):
                    gbuf[_token_chunk(r0s[j], s, EXPERT_TILE), :] = jnp.where(
                        valid, ys[j][:, s * LANES:(s + 1) * LANES], xss[j][s])
            return carry

        lax.fori_loop(0, functools.reduce(jnp.maximum, ntiles), tile, 0)

    @pl.when(k >= nd + ne)
    def _():
        t0 = (k - nd - ne) * MOE_SUB

        def body(i, carry):
            for u in range(ROW_UNROLL):
                t = i * ROW_UNROLL + u
                y1 = gbuf[_token_rows(pos_ref[base + t0 + t]), :]
                y2 = gbuf[_token_rows(pos_ref[base + seq + t0 + t]), :]
                stage[_token_rows(t * SUBLANES), :] = wts_ref[base + t0 + t] * y1 + wts_ref[base + seq + t0 + t] * y2
            return carry

        lax.fori_loop(0, MOE_SUB // ROW_UNROLL, body, 0)
        for s in range(SUBLANES):
            sl = slice(s * LANES, (s + 1) * LANES)
            o_ref[:, sl] = x_ref[:, sl] + mod_ref[5:6, sl] * stage[_token_chunk(0, s, MOE_SUB), :]
        if final_norm:
            x2 = o_ref[...]
            ms = jnp.mean(x2 * x2, axis=-1, keepdims=True)
            o_ref[...] = x2 * lax.rsqrt(ms + RMS_EPS) * fg_ref[...]


def _moe(h2r, pos, wts, tbl, wgu, wd, x1, mod, final_g, final_norm):
    b, s, d = x1.shape
    nd = s // MOE_SUB
    ne = N_EXPERTS // EXPERTS_PER_STEP
    clip = lambda v, hi: jnp.minimum(jnp.maximum(v, 0), hi)
    tok_in = lambda i, k, *_: (i, clip(k, nd - 1), 0)
    tok_out = lambda i, k, *_: (i, clip(k - nd - ne, nd - 1), 0)
    expert = lambda i, k, *_: (clip(k - nd, ne - 1), 0, 0)
    grid_spec = pltpu.PrefetchScalarGridSpec(
        num_scalar_prefetch=3,
        grid=(b, nd + ne + nd),
        in_specs=[
            pl.BlockSpec((None, MOE_SUB * SUBLANES, LANES), tok_in),
            pl.BlockSpec((EXPERTS_PER_STEP, d, 2 * EXPERT_HIDDEN), expert),
            pl.BlockSpec((EXPERTS_PER_STEP, EXPERT_HIDDEN, d), expert),
            pl.BlockSpec((None, MOE_SUB, d), tok_out),
            pl.BlockSpec((None, N_MOD, d), lambda i, k, *_: (i, 0, 0)),
            pl.BlockSpec((1, d), lambda i, k, *_: (0, 0)),
        ],
        out_specs=pl.BlockSpec((None, MOE_SUB, d), tok_out),
        scratch_shapes=[
            pltpu.VMEM(((2 * s + EXPERT_TILE) * SUBLANES, LANES), F32),
            pltpu.VMEM((MOE_SUB * SUBLANES, LANES), F32),
        ],
    )
    return pl.pallas_call(
        functools.partial(_moe_kernel, final_norm=final_norm, seq=s),
        grid_spec=grid_spec,
        out_shape=jax.ShapeDtypeStruct((b, s, d), F32),
        compiler_params=pltpu.CompilerParams(
            dimension_semantics=("arbitrary", "arbitrary"), vmem_limit_bytes=MOE_VMEM_LIMIT),
        name="moe",
    )(pos, wts, tbl, h2r, wgu, wd, x1, mod, final_g)


def _dispatch_tables(route, counts):
    ids = route[:, 0:2, :].astype(jnp.int32)
    ranks = route[:, 4:6, :].astype(jnp.int32)
    cnt = counts[:, 0, :N_EXPERTS].astype(jnp.int32)
    starts = jnp.cumsum(cnt, axis=1) - cnt
    hot = ids[..., None] == jnp.arange(N_EXPERTS, dtype=jnp.int32)
    pos = jnp.sum(jnp.where(hot, starts[:, None, None, :], 0), axis=-1) + ranks
    row0 = (pos * SUBLANES).reshape(-1)
    wts = route[:, 2:4, :].reshape(-1)
    tbl = jnp.stack([starts, cnt], axis=-1).reshape(-1)
    return row0, wts, tbl


def _router_weights(rg_w, rg_b, re_w, re_b):
    d = rg_w.shape[0]
    pad = LANES - N_EXPERTS - N_GROUPS
    w = jnp.concatenate([re_w, rg_w, jnp.zeros((d, pad), F32)], axis=1)
    bias = jnp.concatenate([re_b, rg_b, jnp.zeros((pad,), F32)]).reshape(1, LANES)
    w_hi = w.astype(BF16)
    w_lo = (w - w_hi.astype(F32)).astype(BF16)
    return jnp.concatenate([w_hi, w_lo], axis=1), bias


def kernel(x, c, ada_w, ada_b, norm1_g, w_in, pool_w, pool_scale, conv_w, conv_b, conv_ln_g, conv_ln_b,
           w_out, norm2_g, router_group_w, router_group_b, router_expert_w, router_expert_b,
           expert_w_gate, expert_w_up, expert_w_down, final_g):
    depth = ada_w.shape[0]
    b = x.shape[0]
    mods = _adaln(c, ada_w, ada_b).reshape(depth, b, N_MOD, D_MODEL)
    head = lax.broadcasted_iota(jnp.int32, (LN_BLOCK, LN_BLOCK), 0) // CONV_HEAD_DIM
    head_t = lax.broadcasted_iota(jnp.int32, (LN_BLOCK, LN_BLOCK), 1) // CONV_HEAD_DIM
    avg = jnp.where(head == head_t, 1.0 / CONV_HEAD_DIM, 0.0).astype(BF16)
    earlier = (lax.broadcasted_iota(jnp.int32, (TAIL_ROWS, TAIL_ROWS), 1)
               < lax.broadcasted_iota(jnp.int32, (TAIL_ROWS, TAIL_ROWS), 0))
    tril = jnp.where(earlier, 1.0, 0.0).astype(BF16)
    row = lambda v: v.reshape(1, -1)
    for l in range(depth):
        win = jnp.swapaxes(w_in[l].astype(BF16).reshape(D_MODEL, -1, PROJ_PIECE), 0, 1)
        wr, br = _router_weights(router_group_w[l], router_group_b[l],
                                       router_expert_w[l], router_expert_b[l])
        x1, h2r, route, counts = _mixer(
            x, mods[l], row(norm1_g[l]), win, pool_w[l].astype(BF16), row(pool_scale[l]),
            conv_w[l], row(conv_b[l]), row(conv_ln_g[l]), row(conv_ln_b[l]), avg,
            w_out[l].astype(BF16), row(norm2_g[l]), wr, br, tril)
        pos, wts, tbl = _dispatch_tables(route, counts)
        wgu = jnp.concatenate([expert_w_gate[l], expert_w_up[l]], axis=-1).astype(BF16)
        x = _moe(h2r, pos, wts, tbl, wgu, expert_w_down[l].astype(BF16), x1, mods[l], row(final_g),
                 final_norm=(l == depth - 1))
    return x
```

```python
import functools
import itertools

import jax
import jax.numpy as jnp
from jax import lax
from jax.experimental import pallas as pl
from jax.experimental.pallas import tpu as pltpu

D_MODEL = 1024
POOL_WIDTH = 512
POOL_GROUPS = 4
POOL_GROUP_DIM = 128
POOL_WINDOWS = (2, 4, 8, 16)
CONV_WIDTH = 512
CONV_HEAD_DIM = 64
CONV_KERNEL = 31
N_GROUPS = 4
EXPERTS_PER_GROUP = 8
N_EXPERTS = 32
EXPERT_HIDDEN = 256
N_MOD = 6
RMS_EPS = 1e-6
LN_EPS = 1e-5

LANES = 128
SUBLANES = 8
POOL_HALO = SUBLANES * len(POOL_WINDOWS)
CONV_HALO = 32
MIX_TILE = 512
CONV_ROWS = 32
LN_BLOCK = 256
TAIL_ROWS = 128
PROJ_PIECE = 256
MOE_SUB = 512
EXPERTS_PER_STEP = 2
EXPERT_TILE = 128
ROW_UNROLL = 16
VMEM_LIMIT = 56 * 1024 * 1024
MOE_VMEM_LIMIT = 60 * 1024 * 1024

F32 = jnp.float32
BF16 = jnp.bfloat16


def _silu(v):
    return v * jax.nn.sigmoid(v)


def _split_bf16(v):
    hi = v.astype(BF16)
    lo = (v - hi.astype(F32)).astype(BF16)
    return hi, lo


def _adaln_kernel(c_ref, w_ref, b_ref, o_ref):
    cond = _silu(c_ref[...])
    o_ref[...] = jnp.dot(cond, w_ref[...], preferred_element_type=F32,
                         precision=lax.Precision.HIGHEST) + b_ref[...]


def _adaln(c, ada_w, ada_b):
    depth, d, n = ada_w.shape
    b = c.shape[0]
    tn = D_MODEL
    return pl.pallas_call(
        _adaln_kernel,
        grid=(depth, n // tn),
        in_specs=[
            pl.BlockSpec((b, d), lambda l, j: (0, 0)),
            pl.BlockSpec((None, d, tn), lambda l, j: (l, 0, j)),
            pl.BlockSpec((None, 1, tn), lambda l, j: (l, 0, j)),
        ],
        out_specs=pl.BlockSpec((None, b, tn), lambda l, j: (l, 0, j)),
        out_shape=jax.ShapeDtypeStruct((depth, b, n), F32),
        compiler_params=pltpu.CompilerParams(dimension_semantics=("arbitrary", "arbitrary")),
        name="adaln",
    )(c, ada_w, ada_b.reshape(depth, 1, n))


def _route(logits, tril_ref, cnt_scr):
    t = logits.shape[0]
    lane = lax.broadcasted_iota(jnp.int32, (t, LANES), 1)
    lanef = lane.astype(F32)
    neg = float("-inf")
    big = float(4 * LANES)
    is_group = (lane >= N_EXPERTS) & (lane < N_EXPERTS + N_GROUPS)
    gl = jnp.where(is_group, logits, neg)
    gmax = jnp.max(gl, axis=-1, keepdims=True)
    gidx = jnp.min(jnp.where(gl == gmax, lanef, big), axis=-1, keepdims=True)
    p_group = 1.0 / jnp.sum(jnp.exp(gl - gmax), axis=-1, keepdims=True)
    gsel = gidx.astype(jnp.int32) - N_EXPERTS
    in_group = (lane < N_EXPERTS) & ((lane >> 3) == gsel)
    el = jnp.where(in_group, logits, neg)
    m1 = jnp.max(el, axis=-1, keepdims=True)
    i1 = jnp.min(jnp.where(el == m1, lanef, big), axis=-1, keepdims=True)
    el2 = jnp.where(lanef == i1, neg, el)
    m2 = jnp.max(el2, axis=-1, keepdims=True)
    i2 = jnp.min(jnp.where(el2 == m2, lanef, big), axis=-1, keepdims=True)
    ratio = jnp.exp(m2 - m1)
    w1 = p_group / (1.0 + ratio)
    w2 = p_group * ratio / (1.0 + ratio)
    hot1 = lanef == i1
    hot2 = lanef == i2
    onehot = jnp.where(hot1, 1.0, 0.0) + jnp.where(hot2, 1.0, 0.0)
    before = jnp.dot(tril_ref[...], onehot.astype(BF16), preferred_element_type=F32) + cnt_scr[...]
    rank1 = jnp.sum(jnp.where(hot1, before, 0.0), axis=-1, keepdims=True)
    rank2 = jnp.sum(jnp.where(hot2, before, 0.0), axis=-1, keepdims=True)
    cnt_scr[...] = cnt_scr[...] + jnp.sum(onehot, axis=0, keepdims=True)
    slab = jnp.zeros((t, LANES), F32)
    for col, val in enumerate((i1, i2, w1, w2, rank1, rank2)):
        slab = jnp.where(lane == col, val, slab)
    return slab


def _mixer_kernel(x_ref, xnext_ref, mod_ref, modnext_ref, n1g_ref, win_ref, poolw_ref, pscale_ref, convw_ref,
                  convb_ref, lng_ref, lnb_ref, avg_ref, wout_ref, n2g_ref, wr_ref, br_ref, tril_ref,
                  xo_ref, h2_ref, route_ref, cnt_ref,
                  uext, pbuf, vsh, cbuf, ycat, hbuf, zbuf, cnt_scr):
    t = MIX_TILE
    i = pl.program_id(0)
    j = pl.program_id(1)
    gate1 = mod_ref[2:3, :]
    shift2, scale2 = mod_ref[3:4, :], mod_ref[4:5, :]

    def norm_in(src_ref, m_ref):
        xin = src_ref[...]
        ms = jnp.mean(xin * xin, axis=-1, keepdims=True)
        gain = n1g_ref[...] * (1.0 + m_ref[1:2, :])
        hbuf[...] = (xin * lax.rsqrt(ms + RMS_EPS) * gain + m_ref[0:1, :]).astype(BF16)

    def project(piece):
        zbuf[piece] = jnp.dot(hbuf[...], win_ref[piece], preferred_element_type=F32)

    def projected(col):
        piece, off = divmod(col, PROJ_PIECE)
        return zbuf[piece, :, off:off + LANES]

    n_pieces = win_ref.shape[0]

    @pl.when((i == 0) & (j == 0))
    def _():
        norm_in(x_ref, mod_ref)
        for piece in range(n_pieces):
            project(piece)

    @pl.when(j == 0)
    def _():
        cnt_scr[...] = jnp.zeros_like(cnt_scr)
        uext[0:POOL_HALO, :] = jnp.zeros((POOL_HALO, POOL_WIDTH), F32)
        vsh[0, 0:CONV_HALO, :] = jnp.zeros((CONV_HALO, CONV_WIDTH), F32)

    @pl.when(j > 0)
    def _():
        uext[0:POOL_HALO, :] = uext[t:t + POOL_HALO, :]
        vsh[0, 0:CONV_HALO, :] = vsh[0, t:t + CONV_HALO, :]

    for p in range(POOL_WIDTH // LANES):
        uext[POOL_HALO:POOL_HALO + t, p * LANES:(p + 1) * LANES] = projected(p * LANES)
    for p in range(CONV_WIDTH // LANES):
        value = projected(POOL_WIDTH + p * LANES)
        gate = projected(POOL_WIDTH + CONV_WIDTH + p * LANES)
        vsh[0, CONV_HALO:CONV_HALO + t, p * LANES:(p + 1) * LANES] = value * jax.nn.sigmoid(gate)

    norm_in(xnext_ref, modnext_ref)
    pieces = iter(range(n_pieces))

    def project_some(n):
        for piece in itertools.islice(pieces, n):
            project(piece)

    pos = j * t + lax.broadcasted_iota(jnp.int32, (t, 1), 0)
    rows = POOL_HALO + t
    for g, w in enumerate(POOL_WINDOWS):
        assert w == 2 << g
        project_some(1)
        half, lo, c0 = w // 2, SUBLANES * (g + 1), g * POOL_GROUP_DIM
        if g == 0:
            level = uext[lo:rows, c0:] + uext[lo - half:rows - half, c0:]
        else:
            level = pbuf[g - 1, lo:rows, c0:] + pbuf[g - 1, lo - half:rows - half, c0:]
        if g + 1 < len(POOL_WINDOWS):
            pbuf[g, lo:rows, c0:] = level
        s = level[POOL_HALO - lo:, 0:POOL_GROUP_DIM]
        cur = uext[POOL_HALO:rows, c0:c0 + POOL_GROUP_DIM]
        inv_count = 1.0 / jnp.minimum(pos + 1, w).astype(F32)
        pooled = s * inv_count - cur
        yp = jnp.dot(pooled.astype(BF16), poolw_ref[g], preferred_element_type=F32)
        ycat[:, c0:c0 + POOL_GROUP_DIM] = (yp * pscale_ref[:, c0:c0 + POOL_GROUP_DIM]).astype(BF16)

    shifted_rows = CONV_HALO + t - SUBLANES
    for r in range(1, SUBLANES):
        if r % 3 == 1:
            project_some(1)
        vsh[r, 0:shifted_rows, :] = vsh[0, r:r + shifted_rows, :]
    project_some(n_pieces)

    def conv_chunk(c, carry):
        r0 = pl.multiple_of(c * CONV_ROWS, CONV_ROWS)
        groups = range(0, CONV_ROWS, SUBLANES)
        accs = [jnp.broadcast_to(convb_ref[...], (SUBLANES, CONV_WIDTH)) for _ in groups]
        for k in range(CONV_KERNEL):
            q, r = divmod(CONV_HALO - (CONV_KERNEL - 1) + k, SUBLANES)
            w = convw_ref[k]
            for n, g in enumerate(groups):
                accs[n] = accs[n] + w * vsh[r, pl.ds(r0 + q * SUBLANES + g, SUBLANES), :]
        for n, g in enumerate(groups):
            cbuf[pl.ds(r0 + g, SUBLANES), :] = accs[n]
        return carry

    lax.fori_loop(0, t // CONV_ROWS, conv_chunk, 0)

    blocks = [slice(r, r + TAIL_ROWS) for r in range(0, t, TAIL_ROWS)]

    for rs in blocks:
        for c in range(CONV_WIDTH // LN_BLOCK):
            c0 = c * LN_BLOCK
            yc = cbuf[rs, c0:c0 + LN_BLOCK]
            hi, lo = _split_bf16(yc)
            mu = (jnp.dot(hi, avg_ref[...], preferred_element_type=F32)
                  + jnp.dot(lo, avg_ref[...], preferred_element_type=F32))
            dlt = yc - mu
            hi, lo = _split_bf16(dlt * dlt)
            var = (jnp.dot(hi, avg_ref[...], preferred_element_type=F32)
                   + jnp.dot(lo, avg_ref[...], preferred_element_type=F32))
            yn = dlt * lax.rsqrt(var + LN_EPS) * lng_ref[:, c0:c0 + LN_BLOCK] + lnb_ref[:, c0:c0 + LN_BLOCK]
            ycat[rs, POOL_WIDTH + c0:POOL_WIDTH + c0 + LN_BLOCK] = _silu(yn).astype(BF16)

    x1s = []
    for rs in blocks:
        x1 = x_ref[rs, :] + gate1 * jnp.dot(ycat[rs, :], wout_ref[...], preferred_element_type=F32)
        xo_ref[rs, :] = x1
        x1s.append(x1)

    h2s = []
    for rs, x1 in zip(blocks, x1s):
        ms2 = jnp.mean(x1 * x1, axis=-1, keepdims=True)
        h2 = x1 * lax.rsqrt(ms2 + RMS_EPS) * (n2g_ref[...] * (1.0 + scale2)) + shift2
        for s in range(SUBLANES):
            h2_ref[pl.ds(rs.start * SUBLANES + s, TAIL_ROWS, stride=SUBLANES), :] = h2[:, s * LANES:(s + 1) * LANES]
        h2s.append(h2)

    logits = []
    for h2 in h2s:
        hi, lo = _split_bf16(h2)
        prod = (jnp.dot(hi, wr_ref[...], preferred_element_type=F32)
                + jnp.dot(lo, wr_ref[...], preferred_element_type=F32))
        logits.append(prod[:, 0:LANES] + prod[:, LANES:2 * LANES] + br_ref[...])

    for rs, lg in zip(blocks, logits):
        route_ref[:, rs] = _route(lg, tril_ref, cnt_scr).T[0:SUBLANES, :]
    cnt_ref[...] = cnt_scr[...]


def _mixer(x, mod, n1g, win, poolw, pscale, convw, convb, lng, lnb, avg, wout, n2g, wr, br, tril):
    b, s, d = x.shape
    t = MIX_TILE
    assert d == SUBLANES * LANES and win.shape[1:] == (d, PROJ_PIECE)
    nj = s // t
    succ = lambda i, j: (jnp.minimum(i + (j + 1) // nj, b - 1), jnp.where(i + (j + 1) // nj < b, (j + 1) % nj, j))
    const = lambda shape: pl.BlockSpec(shape, lambda i, j: (0,) * len(shape))
    tok = lambda width: pl.BlockSpec((None, t, width), lambda i, j: (i, j, 0))
    return pl.pallas_call(
        _mixer_kernel,
        grid=(b, s // t),
        in_specs=[
            tok(d),
            pl.BlockSpec((None, t, d), lambda i, j: (*succ(i, j), 0)),
            pl.BlockSpec((None, N_MOD, d), lambda i, j: (i, 0, 0)),
            pl.BlockSpec((None, N_MOD, d), lambda i, j: (succ(i, j)[0], 0, 0)),
            const((1, d)),
            const(win.shape), const(poolw.shape), const((1, POOL_WIDTH)),
            const(convw.shape), const((1, CONV_WIDTH)), const((1, CONV_WIDTH)), const((1, CONV_WIDTH)),
            const(avg.shape), const(wout.shape), const((1, d)),
            const(wr.shape), const((1, LANES)), const(tril.shape),
        ],
        out_specs=[
            tok(d),
            pl.BlockSpec((None, t * SUBLANES, LANES), lambda i, j: (i, j, 0)),
            pl.BlockSpec((None, SUBLANES, t), lambda i, j: (i, 0, j)),
            pl.BlockSpec((None, 1, LANES), lambda i, j: (i, 0, 0)),
        ],
        out_shape=[
            jax.ShapeDtypeStruct((b, s, d), F32),
            jax.ShapeDtypeStruct((b, s * SUBLANES, LANES), F32),
            jax.ShapeDtypeStruct((b, SUBLANES, s), F32),
            jax.ShapeDtypeStruct((b, 1, LANES), F32),
        ],
        scratch_shapes=[
            pltpu.VMEM((POOL_HALO + t, POOL_WIDTH), F32),
            pltpu.VMEM((len(POOL_WINDOWS) - 1, POOL_HALO + t, POOL_WIDTH), F32),
            pltpu.VMEM((SUBLANES, CONV_HALO + t, CONV_WIDTH), F32),
            pltpu.VMEM((t, CONV_WIDTH), F32),
            pltpu.VMEM((t, d), BF16),
            pltpu.VMEM((t, d), BF16),
            pltpu.VMEM((win.shape[0], t, PROJ_PIECE), F32),
            pltpu.VMEM((1, LANES), F32),
        ],
        compiler_params=pltpu.CompilerParams(
            dimension_semantics=("arbitrary", "arbitrary"), vmem_limit_bytes=VMEM_LIMIT),
        name="mixer",
    )(x, x, mod, mod, n1g, win, poolw, pscale, convw, convb, lng, lnb, avg, wout, n2g, wr, br, tril)


def _token_rows(row0):
    return pl.ds(pl.multiple_of(row0, SUBLANES), SUBLANES)


def _token_chunk(tok0, s, n):
    return pl.ds(tok0 * SUBLANES + s, n, stride=SUBLANES)


def _moe_kernel(pos_ref, wts_ref, tbl_ref, h_ref, wg_ref, wu_ref, wd_ref, x_ref, mod_ref, fg_ref, o_ref,
                gbuf, stage, *, final_norm, seq):
    nd = seq // MOE_SUB
    ne = N_EXPERTS // EXPERTS_PER_STEP
    b = pl.program_id(0)
    k = pl.program_id(1)
    base = b * (2 * seq)

    @pl.when(k == 0)
    def _():
        gbuf[pl.ds(2 * seq * SUBLANES, EXPERT_TILE * SUBLANES), :] = jnp.zeros((EXPERT_TILE * SUBLANES, LANES), F32)

    @pl.when(k < nd)
    def _():
        t0 = k * MOE_SUB

        def body(i, carry):
            for u in range(ROW_UNROLL):
                t = i * ROW_UNROLL + u
                row = h_ref[_token_rows(t * SUBLANES), :]
                gbuf[_token_rows(pos_ref[base + t0 + t]), :] = row
                gbuf[_token_rows(pos_ref[base + seq + t0 + t]), :] = row
            return carry

        lax.fori_loop(0, MOE_SUB // ROW_UNROLL, body, 0)

    @pl.when((k >= nd) & (k < nd + ne))
    def _():
        starts, counts, ntiles = [], [], []
        for j in range(EXPERTS_PER_STEP):
            e = (k - nd) * EXPERTS_PER_STEP + j
            starts.append(tbl_ref[(b * N_EXPERTS + e) * 2])
            counts.append(tbl_ref[(b * N_EXPERTS + e) * 2 + 1])
            ntiles.append((counts[j] + EXPERT_TILE - 1) // EXPERT_TILE)

        def tile(i, carry):
            r0s = [jnp.where(i < ntiles[j], starts[j] + i * EXPERT_TILE, 2 * seq) for j in range(EXPERTS_PER_STEP)]
            xss = [[gbuf[_token_chunk(r0s[j], s, EXPERT_TILE), :] for s in range(SUBLANES)]
                   for j in range(EXPERTS_PER_STEP)]
            xbs = [jnp.concatenate(xss[j], axis=-1).astype(BF16) for j in range(EXPERTS_PER_STEP)]
            gates = [jnp.dot(xbs[j], wg_ref[j], preferred_element_type=F32) for j in range(EXPERTS_PER_STEP)]
            ups = [jnp.dot(xbs[j], wu_ref[j], preferred_element_type=F32) for j in range(EXPERTS_PER_STEP)]
            hids = [(_silu(g) * u).astype(BF16) for g, u in zip(gates, ups)]
            ys = [jnp.dot(hids[j], wd_ref[j], preferred_element_type=F32) for j in range(EXPERTS_PER_STEP)]
            for j in range(EXPERTS_PER_STEP):
                valid = lax.broadcasted_iota(jnp.int32, (EXPERT_TILE, 1), 0) + i * EXPERT_TILE < counts[j]
                for s in range(SUBLANES):
                    gbuf[_token_chunk(r0s[j], s, EXPERT_TILE), :] = jnp.where(
                        valid, ys[j][:, s * LANES:(s + 1) * LANES], xss[j][s])
            return carry

        lax.fori_loop(0, functools.reduce(jnp.maximum, ntiles), tile, 0)

    @pl.when(k >= nd + ne)
    def _():
        t0 = (k - nd - ne) * MOE_SUB

        def body(i, carry):
            for u in range(ROW_UNROLL):
                t = i * ROW_UNROLL + u
                y1 = gbuf[_token_rows(pos_ref[base + t0 + t]), :]
                y2 = gbuf[_token_rows(pos_ref[base + seq + t0 + t]), :]
                stage[_token_rows(t * SUBLANES), :] = wts_ref[base + t0 + t] * y1 + wts_ref[base + seq + t0 + t] * y2
            return carry

        lax.fori_loop(0, MOE_SUB // ROW_UNROLL, body, 0)
        for s in range(SUBLANES):
            sl = slice(s * LANES, (s + 1) * LANES)
            o_ref[:, sl] = x_ref[:, sl] + mod_ref[5:6, sl] * stage[_token_chunk(0, s, MOE_SUB), :]
        if final_norm:
            x2 = o_ref[...]
            ms = jnp.mean(x2 * x2, axis=-1, keepdims=True)
            o_ref[...] = x2 * lax.rsqrt(ms + RMS_EPS) * fg_ref[...]


def _moe(h2r, pos, wts, tbl, wg, wu, wd, layer, x1, mod, final_g, final_norm):
    b, s, d = x1.shape
    nd = s // MOE_SUB
    ne = N_EXPERTS // EXPERTS_PER_STEP
    clip = lambda v, hi: jnp.minimum(jnp.maximum(v, 0), hi)
    tok_in = lambda i, k, *_: (i, clip(k, nd - 1), 0)
    tok_out = lambda i, k, *_: (i, clip(k - nd - ne, nd - 1), 0)
    expert = lambda i, k, *_: (layer, clip(k - nd, ne - 1), 0, 0)
    grid_spec = pltpu.PrefetchScalarGridSpec(
        num_scalar_prefetch=3,
        grid=(b, nd + ne + nd),
        in_specs=[
            pl.BlockSpec((None, MOE_SUB * SUBLANES, LANES), tok_in),
            pl.BlockSpec((None, EXPERTS_PER_STEP, d, EXPERT_HIDDEN), expert),
            pl.BlockSpec((None, EXPERTS_PER_STEP, d, EXPERT_HIDDEN), expert),
            pl.BlockSpec((None, EXPERTS_PER_STEP, EXPERT_HIDDEN, d), expert),
            pl.BlockSpec((None, MOE_SUB, d), tok_out),
            pl.BlockSpec((None, N_MOD, d), lambda i, k, *_: (i, 0, 0)),
            pl.BlockSpec((1, d), lambda i, k, *_: (0, 0)),
        ],
        out_specs=pl.BlockSpec((None, MOE_SUB, d), tok_out),
        scratch_shapes=[
            pltpu.VMEM(((2 * s + EXPERT_TILE) * SUBLANES, LANES), F32),
            pltpu.VMEM((MOE_SUB * SUBLANES, LANES), F32),
        ],
    )
    return pl.pallas_call(
        functools.partial(_moe_kernel, final_norm=final_norm, seq=s),
        grid_spec=grid_spec,
        out_shape=jax.ShapeDtypeStruct((b, s, d), F32),
        compiler_params=pltpu.CompilerParams(
            dimension_semantics=("arbitrary", "arbitrary"), vmem_limit_bytes=MOE_VMEM_LIMIT),
        name="moe",
    )(pos, wts, tbl, h2r, wg, wu, wd, x1, mod, final_g)


def _dispatch_tables(route, counts):
    ids = route[:, 0:2, :].astype(jnp.int32)
    ranks = route[:, 4:6, :].astype(jnp.int32)
    cnt = counts[:, 0, :N_EXPERTS].astype(jnp.int32)
    starts = jnp.cumsum(cnt, axis=1) - cnt
    hot = ids[..., None] == jnp.arange(N_EXPERTS, dtype=jnp.int32)
    pos = jnp.sum(jnp.where(hot, starts[:, None, None, :], 0), axis=-1) + ranks
    row0 = (pos * SUBLANES).reshape(-1)
    wts = route[:, 2:4, :].reshape(-1)
    tbl = jnp.stack([starts, cnt], axis=-1).reshape(-1)
    return row0, wts, tbl


def _router_weights(rg_w, rg_b, re_w, re_b):
    d = rg_w.shape[0]
    pad = LANES - N_EXPERTS - N_GROUPS
    w = jnp.concatenate([re_w, rg_w, jnp.zeros((d, pad), F32)], axis=1)
    bias = jnp.concatenate([re_b, rg_b, jnp.zeros((pad,), F32)]).reshape(1, LANES)
    w_hi = w.astype(BF16)
    w_lo = (w - w_hi.astype(F32)).astype(BF16)
    return jnp.concatenate([w_hi, w_lo], axis=1), bias


def kernel(x, c, ada_w, ada_b, norm1_g, w_in, pool_w, pool_scale, conv_w, conv_b, conv_ln_g, conv_ln_b,
           w_out, norm2_g, router_group_w, router_group_b, router_expert_w, router_expert_b,
           expert_w_gate, expert_w_up, expert_w_down, final_g):
    depth = ada_w.shape[0]
    b = x.shape[0]
    mods = _adaln(c, ada_w, ada_b).reshape(depth, b, N_MOD, D_MODEL)
    head = lax.broadcasted_iota(jnp.int32, (LN_BLOCK, LN_BLOCK), 0) // CONV_HEAD_DIM
    head_t = lax.broadcasted_iota(jnp.int32, (LN_BLOCK, LN_BLOCK), 1) // CONV_HEAD_DIM
    avg = jnp.where(head == head_t, 1.0 / CONV_HEAD_DIM, 0.0).astype(BF16)
    earlier = (lax.broadcasted_iota(jnp.int32, (TAIL_ROWS, TAIL_ROWS), 1)
               < lax.broadcasted_iota(jnp.int32, (TAIL_ROWS, TAIL_ROWS), 0))
    tril = jnp.where(earlier, 1.0, 0.0).astype(BF16)
    row = lambda v: v.reshape(1, -1)
    wg, wu, wd = expert_w_gate.astype(BF16), expert_w_up.astype(BF16), expert_w_down.astype(BF16)
    for l in range(depth):
        win = jnp.swapaxes(w_in[l].astype(BF16).reshape(D_MODEL, -1, PROJ_PIECE), 0, 1)
        wr, br = _router_weights(router_group_w[l], router_group_b[l],
                                       router_expert_w[l], router_expert_b[l])
        x1, h2r, route, counts = _mixer(
            x, mods[l], row(norm1_g[l]), win, pool_w[l].astype(BF16), row(pool_scale[l]),
            jnp.broadcast_to(conv_w[l][:, None, :], (CONV_KERNEL, SUBLANES, CONV_WIDTH)),
            row(conv_b[l]), row(conv_ln_g[l]), row(conv_ln_b[l]), avg,
            w_out[l].astype(BF16), row(norm2_g[l]), wr, br, tril)
        pos, wts, tbl = _dispatch_tables(route, counts)
        x = _moe(h2r, pos, wts, tbl, wg, wu, wd, l, x1, mods[l], row(final_g),
                 final_norm=(l == depth - 1))
    return x
```

```python
import functools
import itertools

import jax
import jax.numpy as jnp
from jax import lax
from jax.experimental import pallas as pl
from jax.experimental.pallas import tpu as pltpu

D_MODEL = 1024
POOL_WIDTH = 512
POOL_GROUPS = 4
POOL_GROUP_DIM = 128
POOL_WINDOWS = (2, 4, 8, 16)
CONV_WIDTH = 512
CONV_HEAD_DIM = 64
CONV_KERNEL = 31
N_GROUPS = 4
EXPERTS_PER_GROUP = 8
N_EXPERTS = 32
EXPERT_HIDDEN = 256
N_MOD = 6
RMS_EPS = 1e-6
LN_EPS = 1e-5

LANES = 128
SUBLANES = 8
POOL_HALO = SUBLANES * len(POOL_WINDOWS)
CONV_HALO = 32
MIX_TILE = 512
CONV_ROWS = 32
LN_BLOCK = 256
WEIGHT_SPLIT = 2
TAIL_ROWS = 128
PROJ_PIECE = 256
MOE_SUB = 512
EXPERTS_PER_STEP = 2
EXPERT_TILE = 128
ROW_UNROLL = 16
VMEM_LIMIT = 56 * 1024 * 1024
MOE_VMEM_LIMIT = 60 * 1024 * 1024

F32 = jnp.float32
BF16 = jnp.bfloat16


def _silu(v):
    return v * jax.nn.sigmoid(v)


def _split_bf16(v):
    hi = v.astype(BF16)
    lo = (v - hi.astype(F32)).astype(BF16)
    return hi, lo


def _adaln_kernel(c_ref, w_ref, b_ref, o_ref):
    cond = _silu(c_ref[...])
    o_ref[...] = jnp.dot(cond, w_ref[...], preferred_element_type=F32,
                         precision=lax.Precision.HIGHEST) + b_ref[...]


def _adaln(c, ada_w, ada_b):
    depth, d, n = ada_w.shape
    b = c.shape[0]
    tn = D_MODEL
    return pl.pallas_call(
        _adaln_kernel,
        grid=(depth, n // tn),
        in_specs=[
            pl.BlockSpec((b, d), lambda l, j: (0, 0)),
            pl.BlockSpec((None, d, tn), lambda l, j: (l, 0, j)),
            pl.BlockSpec((None, 1, tn), lambda l, j: (l, 0, j)),
        ],
        out_specs=pl.BlockSpec((None, b, tn), lambda l, j: (l, 0, j)),
        out_shape=jax.ShapeDtypeStruct((depth, b, n), F32),
        compiler_params=pltpu.CompilerParams(dimension_semantics=("arbitrary", "arbitrary")),
        name="adaln",
    )(c, ada_w, ada_b.reshape(depth, 1, n))


def _route(logits, tril_ref, cnt_scr):
    t = logits.shape[0]
    lane = lax.broadcasted_iota(jnp.int32, (t, LANES), 1)
    lanef = lane.astype(F32)
    neg = float("-inf")
    big = float(4 * LANES)
    is_group = (lane >= N_EXPERTS) & (lane < N_EXPERTS + N_GROUPS)
    gl = jnp.where(is_group, logits, neg)
    gmax = jnp.max(gl, axis=-1, keepdims=True)
    gidx = jnp.min(jnp.where(gl == gmax, lanef, big), axis=-1, keepdims=True)
    p_group = 1.0 / jnp.sum(jnp.exp(gl - gmax), axis=-1, keepdims=True)
    gsel = gidx.astype(jnp.int32) - N_EXPERTS
    in_group = (lane < N_EXPERTS) & ((lane >> 3) == gsel)
    el = jnp.where(in_group, logits, neg)
    m1 = jnp.max(el, axis=-1, keepdims=True)
    i1 = jnp.min(jnp.where(el == m1, lanef, big), axis=-1, keepdims=True)
    el2 = jnp.where(lanef == i1, neg, el)
    m2 = jnp.max(el2, axis=-1, keepdims=True)
    i2 = jnp.min(jnp.where(el2 == m2, lanef, big), axis=-1, keepdims=True)
    ratio = jnp.exp(m2 - m1)
    w1 = p_group / (1.0 + ratio)
    w2 = p_group * ratio / (1.0 + ratio)
    hot1 = lanef == i1
    hot2 = lanef == i2
    onehot = jnp.where(hot1, 1.0, 0.0) + jnp.where(hot2, 1.0, 0.0)
    before = jnp.dot(tril_ref[...], onehot.astype(BF16), preferred_element_type=F32) + cnt_scr[...]
    rank1 = jnp.sum(jnp.where(hot1, before, 0.0), axis=-1, keepdims=True)
    rank2 = jnp.sum(jnp.where(hot2, before, 0.0), axis=-1, keepdims=True)
    cnt_scr[...] = cnt_scr[...] + jnp.sum(onehot, axis=0, keepdims=True)
    slab = jnp.zeros((t, LANES), F32)
    for col, val in enumerate((i1, i2, w1, w2, rank1, rank2)):
        slab = jnp.where(lane == col, val, slab)
    return slab


def _mixer_kernel(x_ref, xnext_ref, mod_ref, modnext_ref, n1g_ref, win_ref, poolw_ref, pscale_ref, convw_ref,
                  convb_ref, lng_ref, lnb_ref, avg_ref, wout_ref, n2g_ref, wr_ref, br_ref, tril_ref,
                  ewg_ref, ewu_ref, ewd_ref,
                  xo_ref, h2_ref, route_ref, cnt_ref, wgu_ref, wd_ref,
                  uext, pbuf, vsh, cbuf, ycat, hbuf, zbuf, cnt_scr):
    t = MIX_TILE
    i = pl.program_id(0)
    j = pl.program_id(1)

    wgu_ref[:, 0:EXPERT_HIDDEN] = ewg_ref[...].astype(BF16)
    wgu_ref[:, EXPERT_HIDDEN:] = ewu_ref[...].astype(BF16)
    wd_ref[...] = ewd_ref[...].astype(BF16)

    gate1 = mod_ref[2:3, :]
    shift2, scale2 = mod_ref[3:4, :], mod_ref[4:5, :]

    def norm_in(src_ref, m_ref):
        xin = src_ref[...]
        ms = jnp.mean(xin * xin, axis=-1, keepdims=True)
        gain = n1g_ref[...] * (1.0 + m_ref[1:2, :])
        hbuf[...] = (xin * lax.rsqrt(ms + RMS_EPS) * gain + m_ref[0:1, :]).astype(BF16)

    def project(piece):
        zbuf[piece] = jnp.dot(hbuf[...], win_ref[piece], preferred_element_type=F32)

    def projected(col):
        piece, off = divmod(col, PROJ_PIECE)
        return zbuf[piece, :, off:off + LANES]

    n_pieces = win_ref.shape[0]

    @pl.when((i == 0) & (j == 0))
    def _():
        norm_in(x_ref, mod_ref)
        for piece in range(n_pieces):
            project(piece)

    @pl.when(j == 0)
    def _():
        cnt_scr[...] = jnp.zeros_like(cnt_scr)
        uext[0:POOL_HALO, :] = jnp.zeros((POOL_HALO, POOL_WIDTH), F32)
        vsh[0, 0:CONV_HALO, :] = jnp.zeros((CONV_HALO, CONV_WIDTH), F32)

    @pl.when(j > 0)
    def _():
        uext[0:POOL_HALO, :] = uext[t:t + POOL_HALO, :]
        vsh[0, 0:CONV_HALO, :] = vsh[0, t:t + CONV_HALO, :]

    for p in range(POOL_WIDTH // LANES):
        uext[POOL_HALO:POOL_HALO + t, p * LANES:(p + 1) * LANES] = projected(p * LANES)
    for p in range(CONV_WIDTH // LANES):
        value = projected(POOL_WIDTH + p * LANES)
        gate = projected(POOL_WIDTH + CONV_WIDTH + p * LANES)
        vsh[0, CONV_HALO:CONV_HALO + t, p * LANES:(p + 1) * LANES] = value * jax.nn.sigmoid(gate)

    norm_in(xnext_ref, modnext_ref)
    pieces = iter(range(n_pieces))

    def project_some(n):
        for piece in itertools.islice(pieces, n):
            project(piece)

    pos = j * t + lax.broadcasted_iota(jnp.int32, (t, 1), 0)
    rows = POOL_HALO + t
    for g, w in enumerate(POOL_WINDOWS):
        assert w == 2 << g
        project_some(1)
        half, lo, c0 = w // 2, SUBLANES * (g + 1), g * POOL_GROUP_DIM
        if g == 0:
            level = uext[lo:rows, c0:] + uext[lo - half:rows - half, c0:]
        else:
            level = pbuf[g - 1, lo:rows, c0:] + pbuf[g - 1, lo - half:rows - half, c0:]
        if g + 1 < len(POOL_WINDOWS):
            pbuf[g, lo:rows, c0:] = level
        s = level[POOL_HALO - lo:, 0:POOL_GROUP_DIM]
        cur = uext[POOL_HALO:rows, c0:c0 + POOL_GROUP_DIM]
        inv_count = 1.0 / jnp.minimum(pos + 1, w).astype(F32)
        pooled = s * inv_count - cur
        yp = jnp.dot(pooled.astype(BF16), poolw_ref[g], preferred_element_type=F32)
        ycat[:, c0:c0 + POOL_GROUP_DIM] = (yp * pscale_ref[:, c0:c0 + POOL_GROUP_DIM]).astype(BF16)

    shifted_rows = CONV_HALO + t - SUBLANES
    for r in range(1, SUBLANES):
        if r % 3 == 1:
            project_some(1)
        vsh[r, 0:shifted_rows, :] = vsh[0, r:r + shifted_rows, :]
    project_some(n_pieces)

    def conv_chunk(c, carry):
        r0 = pl.multiple_of(c * CONV_ROWS, CONV_ROWS)
        groups = range(0, CONV_ROWS, SUBLANES)
        accs = [jnp.broadcast_to(convb_ref[...], (SUBLANES, CONV_WIDTH)) for _ in groups]
        for k in range(CONV_KERNEL):
            q, r = divmod(CONV_HALO - (CONV_KERNEL - 1) + k, SUBLANES)
            w = convw_ref[k]
            for n, g in enumerate(groups):
                accs[n] = accs[n] + w * vsh[r, pl.ds(r0 + q * SUBLANES + g, SUBLANES), :]
        for n, g in enumerate(groups):
            cbuf[pl.ds(r0 + g, SUBLANES), :] = accs[n]
        return carry

    lax.fori_loop(0, t // CONV_ROWS, conv_chunk, 0)

    blocks = [slice(r, r + TAIL_ROWS) for r in range(0, t, TAIL_ROWS)]

    for rs in blocks:
        for c in range(CONV_WIDTH // LN_BLOCK):
            c0 = c * LN_BLOCK
            yc = cbuf[rs, c0:c0 + LN_BLOCK]
            hi, lo = _split_bf16(yc)
            mu = (jnp.dot(hi, avg_ref[...], preferred_element_type=F32)
                  + jnp.dot(lo, avg_ref[...], preferred_element_type=F32))
            dlt = yc - mu
            hi, lo = _split_bf16(dlt * dlt)
            var = (jnp.dot(hi, avg_ref[...], preferred_element_type=F32)
                   + jnp.dot(lo, avg_ref[...], preferred_element_type=F32))
            yn = dlt * lax.rsqrt(var + LN_EPS) * lng_ref[:, c0:c0 + LN_BLOCK] + lnb_ref[:, c0:c0 + LN_BLOCK]
            ycat[rs, POOL_WIDTH + c0:POOL_WIDTH + c0 + LN_BLOCK] = _silu(yn).astype(BF16)

    x1s = []
    for rs in blocks:
        x1 = x_ref[rs, :] + gate1 * jnp.dot(ycat[rs, :], wout_ref[...], preferred_element_type=F32)
        xo_ref[rs, :] = x1
        x1s.append(x1)

    h2s = []
    for rs, x1 in zip(blocks, x1s):
        ms2 = jnp.mean(x1 * x1, axis=-1, keepdims=True)
        h2 = x1 * lax.rsqrt(ms2 + RMS_EPS) * (n2g_ref[...] * (1.0 + scale2)) + shift2
        for s in range(SUBLANES):
            h2_ref[pl.ds(rs.start * SUBLANES + s, TAIL_ROWS, stride=SUBLANES), :] = h2[:, s * LANES:(s + 1) * LANES]
        h2s.append(h2)

    logits = []
    for h2 in h2s:
        hi, lo = _split_bf16(h2)
        prod = (jnp.dot(hi, wr_ref[...], preferred_element_type=F32)
                + jnp.dot(lo, wr_ref[...], preferred_element_type=F32))
        logits.append(prod[:, 0:LANES] + prod[:, LANES:2 * LANES] + br_ref[...])

    for rs, lg in zip(blocks, logits):
        route_ref[:, rs] = _route(lg, tril_ref, cnt_scr).T[0:SUBLANES, :]
    cnt_ref[...] = cnt_scr[...]


def _mixer(x, mod, n1g, win, poolw, pscale, convw, convb, lng, lnb, avg, wout, n2g, wr, br, tril,
           layer, ewg, ewu, ewd):
    b, s, d = x.shape
    t = MIX_TILE
    n_exp, hid = ewg.shape[1], ewg.shape[3]
    assert b * (s // t) == WEIGHT_SPLIT * n_exp
    wshare = lambda i, j: ((i * (s // t) + j) // WEIGHT_SPLIT, (i * (s // t) + j) % WEIGHT_SPLIT, 0)
    assert d == SUBLANES * LANES and win.shape[1:] == (d, PROJ_PIECE)
    nj = s // t
    succ = lambda i, j: (jnp.minimum(i + (j + 1) // nj, b - 1), jnp.where(i + (j + 1) // nj < b, (j + 1) % nj, j))
    const = lambda shape: pl.BlockSpec(shape, lambda i, j: (0,) * len(shape))
    tok = lambda width: pl.BlockSpec((None, t, width), lambda i, j: (i, j, 0))
    return pl.pallas_call(
        _mixer_kernel,
        grid=(b, s // t),
        in_specs=[
            tok(d),
            pl.BlockSpec((None, t, d), lambda i, j: (*succ(i, j), 0)),
            pl.BlockSpec((None, N_MOD, d), lambda i, j: (i, 0, 0)),
            pl.BlockSpec((None, N_MOD, d), lambda i, j: (succ(i, j)[0], 0, 0)),
            const((1, d)),
            const(win.shape), const(poolw.shape), const((1, POOL_WIDTH)),
            const(convw.shape), const((1, CONV_WIDTH)), const((1, CONV_WIDTH)), const((1, CONV_WIDTH)),
            const(avg.shape), const(wout.shape), const((1, d)),
            const(wr.shape), const((1, LANES)), const(tril.shape),
            pl.BlockSpec((None, None, d // WEIGHT_SPLIT, hid), lambda i, j: (layer, *wshare(i, j))),
            pl.BlockSpec((None, None, d // WEIGHT_SPLIT, hid), lambda i, j: (layer, *wshare(i, j))),
            pl.BlockSpec((None, None, hid // WEIGHT_SPLIT, d), lambda i, j: (layer, *wshare(i, j))),
        ],
        out_specs=[
            tok(d),
            pl.BlockSpec((None, t * SUBLANES, LANES), lambda i, j: (i, j, 0)),
            pl.BlockSpec((None, SUBLANES, t), lambda i, j: (i, 0, j)),
            pl.BlockSpec((None, 1, LANES), lambda i, j: (i, 0, 0)),
            pl.BlockSpec((None, d // WEIGHT_SPLIT, 2 * hid), wshare),
            pl.BlockSpec((None, hid // WEIGHT_SPLIT, d), wshare),
        ],
        out_shape=[
            jax.ShapeDtypeStruct((b, s, d), F32),
            jax.ShapeDtypeStruct((b, s * SUBLANES, LANES), F32),
            jax.ShapeDtypeStruct((b, SUBLANES, s), F32),
            jax.ShapeDtypeStruct((b, 1, LANES), F32),
            jax.ShapeDtypeStruct((n_exp, d, 2 * hid), BF16),
            jax.ShapeDtypeStruct((n_exp, hid, d), BF16),
        ],
        scratch_shapes=[
            pltpu.VMEM((POOL_HALO + t, POOL_WIDTH), F32),
            pltpu.VMEM((len(POOL_WINDOWS) - 1, POOL_HALO + t, POOL_WIDTH), F32),
            pltpu.VMEM((SUBLANES, CONV_HALO + t, CONV_WIDTH), F32),
            pltpu.VMEM((t, CONV_WIDTH), F32),
            pltpu.VMEM((t, d), BF16),
            pltpu.VMEM((t, d), BF16),
            pltpu.VMEM((win.shape[0], t, PROJ_PIECE), F32),
            pltpu.VMEM((1, LANES), F32),
        ],
        compiler_params=pltpu.CompilerParams(
            dimension_semantics=("arbitrary", "arbitrary"), vmem_limit_bytes=VMEM_LIMIT),
        name="mixer",
    )(x, x, mod, mod, n1g, win, poolw, pscale, convw, convb, lng, lnb, avg, wout, n2g, wr, br, tril,
      ewg, ewu, ewd)


def _token_rows(row0):
    return pl.ds(pl.multiple_of(row0, SUBLANES), SUBLANES)


def _token_chunk(tok0, s, n):
    return pl.ds(tok0 * SUBLANES + s, n, stride=SUBLANES)


def _moe_kernel(pos_ref, wts_ref, tbl_ref, h_ref, wgu_ref, wd_ref, x_ref, mod_ref, fg_ref, o_ref,
                gbuf, stage, *, final_norm, seq):
    nd = seq // MOE_SUB
    ne = N_EXPERTS // EXPERTS_PER_STEP
    b = pl.program_id(0)
    k = pl.program_id(1)
    base = b * (2 * seq)

    @pl.when(k == 0)
    def _():
        gbuf[pl.ds(2 * seq * SUBLANES, EXPERT_TILE * SUBLANES), :] = jnp.zeros((EXPERT_TILE * SUBLANES, LANES), F32)

    @pl.when(k < nd)
    def _():
        t0 = k * MOE_SUB

        def body(i, carry):
            for u in range(ROW_UNROLL):
                t = i * ROW_UNROLL + u
                row = h_ref[_token_rows(t * SUBLANES), :]
                gbuf[_token_rows(pos_ref[base + t0 + t]), :] = row
                gbuf[_token_rows(pos_ref[base + seq + t0 + t]), :] = row
            return carry

        lax.fori_loop(0, MOE_SUB // ROW_UNROLL, body, 0)

    @pl.when((k >= nd) & (k < nd + ne))
    def _():
        starts, counts, ntiles = [], [], []
        for j in range(EXPERTS_PER_STEP):
            e = (k - nd) * EXPERTS_PER_STEP + j
            starts.append(tbl_ref[(b * N_EXPERTS + e) * 2])
            counts.append(tbl_ref[(b * N_EXPERTS + e) * 2 + 1])
            ntiles.append((counts[j] + EXPERT_TILE - 1) // EXPERT_TILE)

        def tile(i, carry):
            r0s = [jnp.where(i < ntiles[j], starts[j] + i * EXPERT_TILE, 2 * seq) for j in range(EXPERTS_PER_STEP)]
            xss = [[gbuf[_token_chunk(r0s[j], s, EXPERT_TILE), :] for s in range(SUBLANES)]
                   for j in range(EXPERTS_PER_STEP)]
            xbs = [jnp.concatenate(xss[j], axis=-1).astype(BF16) for j in range(EXPERTS_PER_STEP)]
            acts = [jnp.dot(xbs[j], wgu_ref[j], preferred_element_type=F32) for j in range(EXPERTS_PER_STEP)]
            hids = [(_silu(a[:, :EXPERT_HIDDEN]) * a[:, EXPERT_HIDDEN:]).astype(BF16) for a in acts]
            ys = [jnp.dot(hids[j], wd_ref[j], preferred_element_type=F32) for j in range(EXPERTS_PER_STEP)]
            for j in range(EXPERTS_PER_STEP):
                valid = lax.broadcasted_iota(jnp.int32, (EXPERT_TILE, 1), 0) + i * EXPERT_TILE < counts[j]
                for s in range(SUBLANES):
                    gbuf[_token_chunk(r0s[j], s, EXPERT_TILE), :] = jnp.where(
                        valid, ys[j][:, s * LANES:(s + 1) * LANES], xss[j][s])
            return carry

        lax.fori_loop(0, functools.reduce(jnp.maximum, ntiles), tile, 0)

    @pl.when(k >= nd + ne)
    def _():
        t0 = (k - nd - ne) * MOE_SUB

        def body(i, carry):
            for u in range(ROW_UNROLL):
                t = i * ROW_UNROLL + u
                y1 = gbuf[_token_rows(pos_ref[base + t0 + t]), :]
                y2 = gbuf[_token_rows(pos_ref[base + seq + t0 + t]), :]
                stage[_token_rows(t * SUBLANES), :] = wts_ref[base + t0 + t] * y1 + wts_ref[base + seq + t0 + t] * y2
            return carry

        lax.fori_loop(0, MOE_SUB // ROW_UNROLL, body, 0)
        for s in range(SUBLANES):
            sl = slice(s * LANES, (s + 1) * LANES)
            o_ref[:, sl] = x_ref[:, sl] + mod_ref[5:6, sl] * stage[_token_chunk(0, s, MOE_SUB), :]
        if final_norm:
            x2 = o_ref[...]
            ms = jnp.mean(x2 * x2, axis=-1, keepdims=True)
            o_ref[...] = x2 * lax.rsqrt(ms + RMS_EPS) * fg_ref[...]


def _moe(h2r, pos, wts, tbl, wgu, wd, x1, mod, final_g, final_norm):
    b, s, d = x1.shape
    nd = s // MOE_SUB
    ne = N_EXPERTS // EXPERTS_PER_STEP
    clip = lambda v, hi: jnp.minimum(jnp.maximum(v, 0), hi)
    tok_in = lambda i, k, *_: (i, clip(k, nd - 1), 0)
    tok_out = lambda i, k, *_: (i, clip(k - nd - ne, nd - 1), 0)
    expert = lambda i, k, *_: (clip(k - nd, ne - 1), 0, 0)
    grid_spec = pltpu.PrefetchScalarGridSpec(
        num_scalar_prefetch=3,
        grid=(b, nd + ne + nd),
        in_specs=[
            pl.BlockSpec((None, MOE_SUB * SUBLANES, LANES), tok_in),
            pl.BlockSpec((EXPERTS_PER_STEP, d, 2 * EXPERT_HIDDEN), expert),
            pl.BlockSpec((EXPERTS_PER_STEP, EXPERT_HIDDEN, d), expert),
            pl.BlockSpec((None, MOE_SUB, d), tok_out),
            pl.BlockSpec((None, N_MOD, d), lambda i, k, *_: (i, 0, 0)),
            pl.BlockSpec((1, d), lambda i, k, *_: (0, 0)),
        ],
        out_specs=pl.BlockSpec((None, MOE_SUB, d), tok_out),
        scratch_shapes=[
            pltpu.VMEM(((2 * s + EXPERT_TILE) * SUBLANES, LANES), F32),
            pltpu.VMEM((MOE_SUB * SUBLANES, LANES), F32),
        ],
    )
    return pl.pallas_call(
        functools.partial(_moe_kernel, final_norm=final_norm, seq=s),
        grid_spec=grid_spec,
        out_shape=jax.ShapeDtypeStruct((b, s, d), F32),
        compiler_params=pltpu.CompilerParams(
            dimension_semantics=("arbitrary", "arbitrary"), vmem_limit_bytes=MOE_VMEM_LIMIT),
        name="moe",
    )(pos, wts, tbl, h2r, wgu, wd, x1, mod, final_g)


def _dispatch_tables(route, counts):
    ids = route[:, 0:2, :].astype(jnp.int32)
    ranks = route[:, 4:6, :].astype(jnp.int32)
    cnt = counts[:, 0, :N_EXPERTS].astype(jnp.int32)
    starts = jnp.cumsum(cnt, axis=1) - cnt
    hot = ids[..., None] == jnp.arange(N_EXPERTS, dtype=jnp.int32)
    pos = jnp.sum(jnp.where(hot, starts[:, None, None, :], 0), axis=-1) + ranks
    row0 = (pos * SUBLANES).reshape(-1)
    wts = route[:, 2:4, :].reshape(-1)
    tbl = jnp.stack([starts, cnt], axis=-1).reshape(-1)
    return row0, wts, tbl


def _router_weights(rg_w, rg_b, re_w, re_b):
    d = rg_w.shape[0]
    pad = LANES - N_EXPERTS - N_GROUPS
    w = jnp.concatenate([re_w, rg_w, jnp.zeros((d, pad), F32)], axis=1)
    bias = jnp.concatenate([re_b, rg_b, jnp.zeros((pad,), F32)]).reshape(1, LANES)
    w_hi = w.astype(BF16)
    w_lo = (w - w_hi.astype(F32)).astype(BF16)
    return jnp.concatenate([w_hi, w_lo], axis=1), bias


def kernel(x, c, ada_w, ada_b, norm1_g, w_in, pool_w, pool_scale, conv_w, conv_b, conv_ln_g, conv_ln_b,
           w_out, norm2_g, router_group_w, router_group_b, router_expert_w, router_expert_b,
           expert_w_gate, expert_w_up, expert_w_down, final_g):
    depth = ada_w.shape[0]
    b = x.shape[0]
    mods = _adaln(c, ada_w, ada_b).reshape(depth, b, N_MOD, D_MODEL)
    head = lax.broadcasted_iota(jnp.int32, (LN_BLOCK, LN_BLOCK), 0) // CONV_HEAD_DIM
    head_t = lax.broadcasted_iota(jnp.int32, (LN_BLOCK, LN_BLOCK), 1) // CONV_HEAD_DIM
    avg = jnp.where(head == head_t, 1.0 / CONV_HEAD_DIM, 0.0).astype(BF16)
    earlier = (lax.broadcasted_iota(jnp.int32, (TAIL_ROWS, TAIL_ROWS), 1)
               < lax.broadcasted_iota(jnp.int32, (TAIL_ROWS, TAIL_ROWS), 0))
    tril = jnp.where(earlier, 1.0, 0.0).astype(BF16)
    row = lambda v: v.reshape(1, -1)
    for l in range(depth):
        win = jnp.swapaxes(w_in[l].astype(BF16).reshape(D_MODEL, -1, PROJ_PIECE), 0, 1)
        wr, br = _router_weights(router_group_w[l], router_group_b[l],
                                       router_expert_w[l], router_expert_b[l])
        x1, h2r, route, counts, wgu, wd = _mixer(
            x, mods[l], row(norm1_g[l]), win, pool_w[l].astype(BF16), row(pool_scale[l]),
            jnp.broadcast_to(conv_w[l][:, None, :], (CONV_KERNEL, SUBLANES, CONV_WIDTH)),
            row(conv_b[l]), row(conv_ln_g[l]), row(conv_ln_b[l]), avg,
            w_out[l].astype(BF16), row(norm2_g[l]), wr, br, tril,
            l, expert_w_gate, expert_w_up, expert_w_down)
        pos, wts, tbl = _dispatch_tables(route, counts)
        x = _moe(h2r, pos, wts, tbl, wgu, wd, x1, mods[l], row(final_g),
                 final_norm=(l == depth - 1))
    return x
```

```python
import functools
import itertools

import jax
import jax.numpy as jnp
from jax import lax
from jax.experimental import pallas as pl
from jax.experimental.pallas import tpu as pltpu

D_MODEL = 1024
POOL_WIDTH = 512
POOL_GROUPS = 4
POOL_GROUP_DIM = 128
POOL_WINDOWS = (2, 4, 8, 16)
CONV_WIDTH = 512
CONV_HEAD_DIM = 64
CONV_KERNEL = 31
N_GROUPS = 4
EXPERTS_PER_GROUP = 8
N_EXPERTS = 32
EXPERT_HIDDEN = 256
N_MOD = 6
RMS_EPS = 1e-6
LN_EPS = 1e-5

LANES = 128
SUBLANES = 8
PACK_ROWS = 4
POOL_HALO = SUBLANES * len(POOL_WINDOWS)
CONV_HALO = 32
MIX_TILE = 512
CONV_ROWS = 32
LN_BLOCK = 256
WEIGHT_SPLIT = 2
OUT_ROWS = 256
TAIL_ROWS = 128
PROJ_PIECE = 256
MOE_SUB = 1024
EXPERTS_PER_STEP = 4
EXPERT_TILE = 128
ROW_UNROLL = 16
VMEM_LIMIT = 56 * 1024 * 1024
MOE_VMEM_LIMIT = 60 * 1024 * 1024

F32 = jnp.float32
BF16 = jnp.bfloat16


def _silu(v):
    return v * jax.nn.sigmoid(v)


def _split_bf16(v):
    hi = v.astype(BF16)
    lo = (v - hi.astype(F32)).astype(BF16)
    return hi, lo


def _adaln_kernel(c_ref, w_ref, b_ref, o_ref):
    cond = _silu(c_ref[...])
    o_ref[...] = jnp.dot(cond, w_ref[...], preferred_element_type=F32,
                         precision=lax.Precision.HIGHEST) + b_ref[...]


def _adaln(c, ada_w, ada_b):
    depth, d, n = ada_w.shape
    b = c.shape[0]
    tn = D_MODEL
    return pl.pallas_call(
        _adaln_kernel,
        grid=(depth, n // tn),
        in_specs=[
            pl.BlockSpec((b, d), lambda l, j: (0, 0)),
            pl.BlockSpec((None, d, tn), lambda l, j: (l, 0, j)),
            pl.BlockSpec((None, 1, tn), lambda l, j: (l, 0, j)),
        ],
        out_specs=pl.BlockSpec((None, b, tn), lambda l, j: (l, 0, j)),
        out_shape=jax.ShapeDtypeStruct((depth, b, n), F32),
        compiler_params=pltpu.CompilerParams(dimension_semantics=("arbitrary", "arbitrary")),
        name="adaln",
    )(c, ada_w, ada_b.reshape(depth, 1, n))


def _route(logits, tril_ref, cnt_scr):
    t = logits.shape[0]
    lane = lax.broadcasted_iota(jnp.int32, (t, LANES), 1)
    lanef = lane.astype(F32)
    neg = float("-inf")
    big = float(4 * LANES)
    is_group = (lane >= N_EXPERTS) & (lane < N_EXPERTS + N_GROUPS)
    gl = jnp.where(is_group, logits, neg)
    gmax = jnp.max(gl, axis=-1, keepdims=True)
    gidx = jnp.min(jnp.where(gl == gmax, lanef, big), axis=-1, keepdims=True)
    p_group = 1.0 / jnp.sum(jnp.exp(gl - gmax), axis=-1, keepdims=True)
    gsel = gidx.astype(jnp.int32) - N_EXPERTS
    in_group = (lane < N_EXPERTS) & ((lane >> 3) == gsel)
    el = jnp.where(in_group, logits, neg)
    m1 = jnp.max(el, axis=-1, keepdims=True)
    i1 = jnp.min(jnp.where(el == m1, lanef, big), axis=-1, keepdims=True)
    el2 = jnp.where(lanef == i1, neg, el)
    m2 = jnp.max(el2, axis=-1, keepdims=True)
    i2 = jnp.min(jnp.where(el2 == m2, lanef, big), axis=-1, keepdims=True)
    ratio = jnp.exp(m2 - m1)
    w1 = p_group / (1.0 + ratio)
    w2 = p_group * ratio / (1.0 + ratio)
    hot1 = lanef == i1
    hot2 = lanef == i2
    onehot = jnp.where(hot1, 1.0, 0.0) + jnp.where(hot2, 1.0, 0.0)
    before = jnp.dot(tril_ref[...], onehot.astype(BF16), preferred_element_type=F32) + cnt_scr[...]
    rank1 = jnp.sum(jnp.where(hot1, before, 0.0), axis=-1, keepdims=True)
    rank2 = jnp.sum(jnp.where(hot2, before, 0.0), axis=-1, keepdims=True)
    cnt_scr[...] = cnt_scr[...] + jnp.sum(onehot, axis=0, keepdims=True)
    slab = jnp.zeros((t, LANES), F32)
    for col, val in enumerate((i1, i2, w1, w2, rank1, rank2)):
        slab = jnp.where(lane == col, val, slab)
    return slab


def _mixer_kernel(x_ref, xnext_ref, mod_ref, modnext_ref, n1g_ref, win_ref, poolw_ref, pscale_ref, convw_ref,
                  convb_ref, lng_ref, lnb_ref, avg_ref, wout_ref, n2g_ref, wr_ref, br_ref, tril_ref,
                  ewg_ref, ewu_ref, ewd_ref,
                  xo_ref, h2_ref, route_ref, cnt_ref, wgu_ref, wd_ref,
                  uext, pbuf, vsh, cbuf, ycat, hbuf, zbuf, cnt_scr):
    t = MIX_TILE
    i = pl.program_id(0)
    j = pl.program_id(1)

    wgu_ref[:, 0:EXPERT_HIDDEN] = ewg_ref[...].astype(BF16)
    wgu_ref[:, EXPERT_HIDDEN:] = ewu_ref[...].astype(BF16)
    wd_ref[...] = ewd_ref[...].astype(BF16)

    gate1 = mod_ref[2:3, :]
    shift2, scale2 = mod_ref[3:4, :], mod_ref[4:5, :]

    def norm_in(src_ref, m_ref):
        xin = src_ref[...]
        ms = jnp.mean(xin * xin, axis=-1, keepdims=True)
        gain = n1g_ref[...] * (1.0 + m_ref[1:2, :])
        hbuf[...] = (xin * lax.rsqrt(ms + RMS_EPS) * gain + m_ref[0:1, :]).astype(BF16)

    def project(piece):
        zbuf[piece] = jnp.dot(hbuf[...], win_ref[piece], preferred_element_type=F32)

    def projected(col):
        piece, off = divmod(col, PROJ_PIECE)
        return zbuf[piece, :, off:off + LANES]

    n_pieces = win_ref.shape[0]

    @pl.when((i == 0) & (j == 0))
    def _():
        norm_in(x_ref, mod_ref)
        for piece in range(n_pieces):
            project(piece)

    @pl.when(j == 0)
    def _():
        cnt_scr[...] = jnp.zeros_like(cnt_scr)
        uext[0:POOL_HALO, :] = jnp.zeros((POOL_HALO, POOL_WIDTH), F32)
        vsh[0, 0:CONV_HALO, :] = jnp.zeros((CONV_HALO, CONV_WIDTH), F32)

    @pl.when(j > 0)
    def _():
        uext[0:POOL_HALO, :] = uext[t:t + POOL_HALO, :]
        vsh[0, 0:CONV_HALO, :] = vsh[0, t:t + CONV_HALO, :]

    for p in range(POOL_WIDTH // LANES):
        uext[POOL_HALO:POOL_HALO + t, p * LANES:(p + 1) * LANES] = projected(p * LANES)
    for p in range(CONV_WIDTH // LANES):
        value = projected(POOL_WIDTH + p * LANES)
        gate = projected(POOL_WIDTH + CONV_WIDTH + p * LANES)
        vsh[0, CONV_HALO:CONV_HALO + t, p * LANES:(p + 1) * LANES] = value * jax.nn.sigmoid(gate)

    norm_in(xnext_ref, modnext_ref)
    pieces = iter(range(n_pieces))

    def project_some(n):
        for piece in itertools.islice(pieces, n):
            project(piece)

    pos = j * t + lax.broadcasted_iota(jnp.int32, (t, 1), 0)
    rows = POOL_HALO + t
    for g, w in enumerate(POOL_WINDOWS):
        assert w == 2 << g
        project_some(1)
        half, lo, c0 = w // 2, SUBLANES * (g + 1), g * POOL_GROUP_DIM
        if g == 0:
            level = uext[lo:rows, c0:] + uext[lo - half:rows - half, c0:]
        else:
            level = pbuf[g - 1, lo:rows, c0:] + pbuf[g - 1, lo - half:rows - half, c0:]
        if g + 1 < len(POOL_WINDOWS):
            pbuf[g, lo:rows, c0:] = level
        s = level[POOL_HALO - lo:, 0:POOL_GROUP_DIM]
        cur = uext[POOL_HALO:rows, c0:c0 + POOL_GROUP_DIM]
        inv_count = 1.0 / jnp.minimum(pos + 1, w).astype(F32)
        pooled = s * inv_count - cur
        yp = jnp.dot(pooled.astype(BF16), poolw_ref[g], preferred_element_type=F32)
        ycat[:, c0:c0 + POOL_GROUP_DIM] = (yp * pscale_ref[:, c0:c0 + POOL_GROUP_DIM]).astype(BF16)

    shifted_rows = CONV_HALO + t - SUBLANES
    for r in range(1, SUBLANES):
        if r % 3 == 1:
            project_some(1)
        vsh[r, 0:shifted_rows, :] = vsh[0, r:r + shifted_rows, :]
    project_some(n_pieces)

    def conv_chunk(c, carry):
        r0 = pl.multiple_of(c * CONV_ROWS, CONV_ROWS)
        groups = range(0, CONV_ROWS, SUBLANES)
        accs = [jnp.broadcast_to(convb_ref[...], (SUBLANES, CONV_WIDTH)) for _ in groups]
        for k in range(CONV_KERNEL):
            q, r = divmod(CONV_HALO - (CONV_KERNEL - 1) + k, SUBLANES)
            w = convw_ref[k]
            for n, g in enumerate(groups):
                accs[n] = accs[n] + w * vsh[r, pl.ds(r0 + q * SUBLANES + g, SUBLANES), :]
        for n, g in enumerate(groups):
            cbuf[pl.ds(r0 + g, SUBLANES), :] = accs[n]
        return carry

    lax.fori_loop(0, t // CONV_ROWS, conv_chunk, 0)

    blocks = [slice(r, r + TAIL_ROWS) for r in range(0, t, TAIL_ROWS)]

    for rs in blocks:
        for c in range(CONV_WIDTH // LN_BLOCK):
            c0 = c * LN_BLOCK
            yc = cbuf[rs, c0:c0 + LN_BLOCK]
            hi, lo = _split_bf16(yc)
            mu = (jnp.dot(hi, avg_ref[...], preferred_element_type=F32)
                  + jnp.dot(lo, avg_ref[...], preferred_element_type=F32))
            dlt = yc - mu
            hi, lo = _split_bf16(dlt * dlt)
            var = (jnp.dot(hi, avg_ref[...], preferred_element_type=F32)
                   + jnp.dot(lo, avg_ref[...], preferred_element_type=F32))
            yn = dlt * lax.rsqrt(var + LN_EPS) * lng_ref[:, c0:c0 + LN_BLOCK] + lnb_ref[:, c0:c0 + LN_BLOCK]
            ycat[rs, POOL_WIDTH + c0:POOL_WIDTH + c0 + LN_BLOCK] = _silu(yn).astype(BF16)

    x1s = []
    for r in range(0, t, OUT_ROWS):
        rs = slice(r, r + OUT_ROWS)
        x1 = x_ref[rs, :] + gate1 * jnp.dot(ycat[rs, :], wout_ref[...], preferred_element_type=F32)
        xo_ref[rs, :] = x1
        x1s.extend(x1[q:q + TAIL_ROWS] for q in range(0, OUT_ROWS, TAIL_ROWS))

    h2s = []
    for rs, x1 in zip(blocks, x1s):
        ms2 = jnp.mean(x1 * x1, axis=-1, keepdims=True)
        h2 = x1 * lax.rsqrt(ms2 + RMS_EPS) * (n2g_ref[...] * (1.0 + scale2)) + shift2
        words = _pack_row(h2)
        for s in range(PACK_ROWS):
            h2_ref[_token_chunk(rs.start, s, TAIL_ROWS), :] = words[:, s * LANES:(s + 1) * LANES]
        h2s.append(h2)

    logits = []
    for h2 in h2s:
        hi, lo = _split_bf16(h2)
        prod = (jnp.dot(hi, wr_ref[...], preferred_element_type=F32)
                + jnp.dot(lo, wr_ref[...], preferred_element_type=F32))
        logits.append(prod[:, 0:LANES] + prod[:, LANES:2 * LANES] + br_ref[...])

    for rs, lg in zip(blocks, logits):
        route_ref[:, rs] = _route(lg, tril_ref, cnt_scr).T[0:SUBLANES, :]
    cnt_ref[...] = cnt_scr[...]


def _mixer(x, mod, n1g, win, poolw, pscale, convw, convb, lng, lnb, avg, wout, n2g, wr, br, tril,
           layer, ewg, ewu, ewd):
    b, s, d = x.shape
    t = MIX_TILE
    n_exp, hid = ewg.shape[1], ewg.shape[3]
    assert b * (s // t) == WEIGHT_SPLIT * n_exp
    wshare = lambda i, j: ((i * (s // t) + j) // WEIGHT_SPLIT, (i * (s // t) + j) % WEIGHT_SPLIT, 0)
    assert d == SUBLANES * LANES and win.shape[1:] == (d, PROJ_PIECE)
    nj = s // t
    succ = lambda i, j: (jnp.minimum(i + (j + 1) // nj, b - 1), jnp.where(i + (j + 1) // nj < b, (j + 1) % nj, j))
    const = lambda shape: pl.BlockSpec(shape, lambda i, j: (0,) * len(shape))
    tok = lambda width: pl.BlockSpec((None, t, width), lambda i, j: (i, j, 0))
    return pl.pallas_call(
        _mixer_kernel,
        grid=(b, s // t),
        in_specs=[
            tok(d),
            pl.BlockSpec((None, t, d), lambda i, j: (*succ(i, j), 0)),
            pl.BlockSpec((None, N_MOD, d), lambda i, j: (i, 0, 0)),
            pl.BlockSpec((None, N_MOD, d), lambda i, j: (succ(i, j)[0], 0, 0)),
            const((1, d)),
            const(win.shape), const(poolw.shape), const((1, POOL_WIDTH)),
            const(convw.shape), const((1, CONV_WIDTH)), const((1, CONV_WIDTH)), const((1, CONV_WIDTH)),
            const(avg.shape), const(wout.shape), const((1, d)),
            const(wr.shape), const((1, LANES)), const(tril.shape),
            pl.BlockSpec((None, None, d // WEIGHT_SPLIT, hid), lambda i, j: (layer, *wshare(i, j))),
            pl.BlockSpec((None, None, d // WEIGHT_SPLIT, hid), lambda i, j: (layer, *wshare(i, j))),
            pl.BlockSpec((None, None, hid // WEIGHT_SPLIT, d), lambda i, j: (layer, *wshare(i, j))),
        ],
        out_specs=[
            tok(d),
            pl.BlockSpec((None, t * PACK_ROWS, LANES), lambda i, j: (i, j, 0)),
            pl.BlockSpec((None, SUBLANES, t), lambda i, j: (i, 0, j)),
            pl.BlockSpec((None, 1, LANES), lambda i, j: (i, 0, 0)),
            pl.BlockSpec((None, d // WEIGHT_SPLIT, 2 * hid), wshare),
            pl.BlockSpec((None, hid // WEIGHT_SPLIT, d), wshare),
        ],
        out_shape=[
            jax.ShapeDtypeStruct((b, s, d), F32),
            jax.ShapeDtypeStruct((b, s * PACK_ROWS, LANES), jnp.int32),
            jax.ShapeDtypeStruct((b, SUBLANES, s), F32),
            jax.ShapeDtypeStruct((b, 1, LANES), F32),
            jax.ShapeDtypeStruct((n_exp, d, 2 * hid), BF16),
            jax.ShapeDtypeStruct((n_exp, hid, d), BF16),
        ],
        scratch_shapes=[
            pltpu.VMEM((POOL_HALO + t, POOL_WIDTH), F32),
            pltpu.VMEM((len(POOL_WINDOWS) - 1, POOL_HALO + t, POOL_WIDTH), F32),
            pltpu.VMEM((SUBLANES, CONV_HALO + t, CONV_WIDTH), F32),
            pltpu.VMEM((t, CONV_WIDTH), F32),
            pltpu.VMEM((t, d), BF16),
            pltpu.VMEM((t, d), BF16),
            pltpu.VMEM((win.shape[0], t, PROJ_PIECE), F32),
            pltpu.VMEM((1, LANES), F32),
        ],
        compiler_params=pltpu.CompilerParams(
            dimension_semantics=("arbitrary", "arbitrary"), vmem_limit_bytes=VMEM_LIMIT),
        name="mixer",
    )(x, x, mod, mod, n1g, win, poolw, pscale, convw, convb, lng, lnb, avg, wout, n2g, wr, br, tril,
      ewg, ewu, ewd)


def _pack_row(v):
    half = v.shape[1] // 2
    return pltpu.pack_elementwise([v[:, :half], v[:, half:]], packed_dtype=BF16)


def _unpack_words(w, index):
    return pltpu.unpack_elementwise(w, index=index, packed_dtype=BF16, unpacked_dtype=F32)


def _token_rows(row0, rows=PACK_ROWS):
    return pl.ds(pl.multiple_of(row0, rows), rows)


def _token_chunk(tok0, s, n, rows=PACK_ROWS):
    return pl.ds(tok0 * rows + s, n, stride=rows)


def _moe_kernel(pos_ref, wts_ref, tbl_ref, h_ref, wgu_ref, wd_ref, x_ref, mod_ref, fg_ref, o_ref,
                gbuf, stage, *, final_norm, seq):
    nd = seq // MOE_SUB
    ne = N_EXPERTS // EXPERTS_PER_STEP
    b = pl.program_id(0)
    k = pl.program_id(1)
    base = b * (2 * seq)

    @pl.when(k == 0)
    def _():
        gbuf[pl.ds(2 * seq * PACK_ROWS, EXPERT_TILE * PACK_ROWS), :] = jnp.zeros(
            (EXPERT_TILE * PACK_ROWS, LANES), jnp.int32)

    @pl.when(k < nd)
    def _():
        t0 = k * MOE_SUB

        def body(i, carry):
            for u in range(ROW_UNROLL):
                t = i * ROW_UNROLL + u
                row = h_ref[_token_rows(t * PACK_ROWS), :]
                gbuf[_token_rows(pos_ref[base + t0 + t]), :] = row
                gbuf[_token_rows(pos_ref[base + seq + t0 + t]), :] = row
            return carry

        lax.fori_loop(0, MOE_SUB // ROW_UNROLL, body, 0)

    @pl.when((k >= nd) & (k < nd + ne))
    def _():
        starts, counts, ntiles = [], [], []
        for j in range(EXPERTS_PER_STEP):
            e = (k - nd) * EXPERTS_PER_STEP + j
            starts.append(tbl_ref[(b * N_EXPERTS + e) * 2])
            counts.append(tbl_ref[(b * N_EXPERTS + e) * 2 + 1])
            ntiles.append((counts[j] + EXPERT_TILE - 1) // EXPERT_TILE)

        def tile(i, carry):
            r0s = [jnp.where(i < ntiles[j], starts[j] + i * EXPERT_TILE, 2 * seq) for j in range(EXPERTS_PER_STEP)]
            xss = [[gbuf[_token_chunk(r0s[j], s, EXPERT_TILE), :] for s in range(PACK_ROWS)]
                   for j in range(EXPERTS_PER_STEP)]
            xbs = [jnp.concatenate([_unpack_words(w, 0) for w in xss[j]] + [_unpack_words(w, 1) for w in xss[j]],
                                   axis=-1).astype(BF16) for j in range(EXPERTS_PER_STEP)]
            acts = [jnp.dot(xbs[j], wgu_ref[j], preferred_element_type=F32) for j in range(EXPERTS_PER_STEP)]
            hids = [(_silu(a[:, :EXPERT_HIDDEN]) * a[:, EXPERT_HIDDEN:]).astype(BF16) for a in acts]
            ys = [jnp.dot(hids[j], wd_ref[j], preferred_element_type=F32) for j in range(EXPERTS_PER_STEP)]
            for j in range(EXPERTS_PER_STEP):
                valid = lax.broadcasted_iota(jnp.int32, (EXPERT_TILE, 1), 0) + i * EXPERT_TILE < counts[j]
                yw = _pack_row(ys[j])
                for s in range(PACK_ROWS):
                    gbuf[_token_chunk(r0s[j], s, EXPERT_TILE), :] = jnp.where(
                        valid, yw[:, s * LANES:(s + 1) * LANES], xss[j][s])
            return carry

        lax.fori_loop(0, functools.reduce(jnp.maximum, ntiles), tile, 0)

    @pl.when(k >= nd + ne)
    def _():
        t0 = (k - nd - ne) * MOE_SUB

        def body(i, carry):
            for u in range(ROW_UNROLL):
                t = i * ROW_UNROLL + u
                y1 = gbuf[_token_rows(pos_ref[base + t0 + t]), :]
                y2 = gbuf[_token_rows(pos_ref[base + seq + t0 + t]), :]
                w1 = wts_ref[base + t0 + t]
                w2 = wts_ref[base + seq + t0 + t]
                for half in range(2):
                    stage[pl.ds(pl.multiple_of(t * SUBLANES, SUBLANES) + half * PACK_ROWS, PACK_ROWS), :] = (
                        w1 * _unpack_words(y1, half) + w2 * _unpack_words(y2, half))
            return carry

        lax.fori_loop(0, MOE_SUB // ROW_UNROLL, body, 0)
        for s in range(SUBLANES):
            sl = slice(s * LANES, (s + 1) * LANES)
            o_ref[:, sl] = x_ref[:, sl] + mod_ref[5:6, sl] * stage[_token_chunk(0, s, MOE_SUB, SUBLANES), :]
        if final_norm:
            x2 = o_ref[...]
            ms = jnp.mean(x2 * x2, axis=-1, keepdims=True)
            o_ref[...] = x2 * lax.rsqrt(ms + RMS_EPS) * fg_ref[...]


def _moe(h2r, pos, wts, tbl, wgu, wd, x1, mod, final_g, final_norm):
    b, s, d = x1.shape
    nd = s // MOE_SUB
    ne = N_EXPERTS // EXPERTS_PER_STEP
    clip = lambda v, hi: jnp.minimum(jnp.maximum(v, 0), hi)
    tok_in = lambda i, k, *_: (i, clip(k, nd - 1), 0)
    tok_out = lambda i, k, *_: (i, clip(k - nd - ne, nd - 1), 0)
    expert = lambda i, k, *_: (clip(k - nd, ne - 1), 0, 0)
    grid_spec = pltpu.PrefetchScalarGridSpec(
        num_scalar_prefetch=3,
        grid=(b, nd + ne + nd),
        in_specs=[
            pl.BlockSpec((None, MOE_SUB * PACK_ROWS, LANES), tok_in),
            pl.BlockSpec((EXPERTS_PER_STEP, d, 2 * EXPERT_HIDDEN), expert),
            pl.BlockSpec((EXPERTS_PER_STEP, EXPERT_HIDDEN, d), expert),
            pl.BlockSpec((None, MOE_SUB, d), tok_out),
            pl.BlockSpec((None, N_MOD, d), lambda i, k, *_: (i, 0, 0)),
            pl.BlockSpec((1, d), lambda i, k, *_: (0, 0)),
        ],
        out_specs=pl.BlockSpec((None, MOE_SUB, d), tok_out),
        scratch_shapes=[
            pltpu.VMEM(((2 * s + EXPERT_TILE) * PACK_ROWS, LANES), jnp.int32),
            pltpu.VMEM((MOE_SUB * SUBLANES, LANES), F32),
        ],
    )
    return pl.pallas_call(
        functools.partial(_moe_kernel, final_norm=final_norm, seq=s),
        grid_spec=grid_spec,
        out_shape=jax.ShapeDtypeStruct((b, s, d), F32),
        compiler_params=pltpu.CompilerParams(
            dimension_semantics=("arbitrary", "arbitrary"), vmem_limit_bytes=MOE_VMEM_LIMIT),
        name="moe",
    )(pos, wts, tbl, h2r, wgu, wd, x1, mod, final_g)


def _dispatch_tables(route, counts):
    ids = route[:, 0:2, :].astype(jnp.int32)
    ranks = route[:, 4:6, :].astype(jnp.int32)
    cnt = counts[:, 0, :N_EXPERTS].astype(jnp.int32)
    starts = jnp.cumsum(cnt, axis=1) - cnt
    hot = ids[..., None] == jnp.arange(N_EXPERTS, dtype=jnp.int32)
    pos = jnp.sum(jnp.where(hot, starts[:, None, None, :], 0), axis=-1) + ranks
    row0 = (pos * PACK_ROWS).reshape(-1)
    wts = route[:, 2:4, :].reshape(-1)
    tbl = jnp.stack([starts, cnt], axis=-1).reshape(-1)
    return row0, wts, tbl


def _router_weights(rg_w, rg_b, re_w, re_b):
    d = rg_w.shape[0]
    pad = LANES - N_EXPERTS - N_GROUPS
    w = jnp.concatenate([re_w, rg_w, jnp.zeros((d, pad), F32)], axis=1)
    bias = jnp.concatenate([re_b, rg_b, jnp.zeros((pad,), F32)]).reshape(1, LANES)
    w_hi = w.astype(BF16)
    w_lo = (w - w_hi.astype(F32)).astype(BF16)
    return jnp.concatenate([w_hi, w_lo], axis=1), bias


def kernel(x, c, ada_w, ada_b, norm1_g, w_in, pool_w, pool_scale, conv_w, conv_b, conv_ln_g, conv_ln_b,
           w_out, norm2_g, router_group_w, router_group_b, router_expert_w, router_expert_b,
           expert_w_gate, expert_w_up, expert_w_down, final_g):
    depth = ada_w.shape[0]
    b = x.shape[0]
    mods = _adaln(c, ada_w, ada_b).reshape(depth, b, N_MOD, D_MODEL)
    head = lax.broadcasted_iota(jnp.int32, (LN_BLOCK, LN_BLOCK), 0) // CONV_HEAD_DIM
    head_t = lax.broadcasted_iota(jnp.int32, (LN_BLOCK, LN_BLOCK), 1) // CONV_HEAD_DIM
    avg = jnp.where(head == head_t, 1.0 / CONV_HEAD_DIM, 0.0).astype(BF16)
    earlier = (lax.broadcasted_iota(jnp.int32, (TAIL_ROWS, TAIL_ROWS), 1)
               < lax.broadcasted_iota(jnp.int32, (TAIL_ROWS, TAIL_ROWS), 0))
    tril = jnp.where(earlier, 1.0, 0.0).astype(BF16)
    row = lambda v: v.reshape(1, -1)
    for l in range(depth):
        win = jnp.swapaxes(w_in[l].astype(BF16).reshape(D_MODEL, -1, PROJ_PIECE), 0, 1)
        wr, br = _router_weights(router_group_w[l], router_group_b[l],
                                       router_expert_w[l], router_expert_b[l])
        x1, h2r, route, counts, wgu, wd = _mixer(
            x, mods[l], row(norm1_g[l]), win, pool_w[l].astype(BF16), row(pool_scale[l]),
            jnp.broadcast_to(conv_w[l][:, None, :], (CONV_KERNEL, SUBLANES, CONV_WIDTH)),
            row(conv_b[l]), row(conv_ln_g[l]), row(conv_ln_b[l]), avg,
            w_out[l].astype(BF16), row(norm2_g[l]), wr, br, tril,
            l, expert_w_gate, expert_w_up, expert_w_down)
        pos, wts, tbl = _dispatch_tables(route, counts)
        x = _moe(h2r, pos, wts, tbl, wgu, wd, x1, mods[l], row(final_g),
                 final_norm=(l == depth - 1))
    return x
```

```python
import functools
import itertools

import jax
import jax.numpy as jnp
from jax import lax
from jax.experimental import pallas as pl
from jax.experimental.pallas import tpu as pltpu

D_MODEL = 1024
POOL_WIDTH = 512
POOL_GROUPS = 4
POOL_GROUP_DIM = 128
POOL_WINDOWS = (2, 4, 8, 16)
CONV_WIDTH = 512
CONV_HEAD_DIM = 64
CONV_KERNEL = 31
N_GROUPS = 4
EXPERTS_PER_GROUP = 8
N_EXPERTS = 32
EXPERT_HIDDEN = 256
N_MOD = 6
RMS_EPS = 1e-6
LN_EPS = 1e-5

LANES = 128
SUBLANES = 8
PACK_ROWS = 4
POOL_HALO = SUBLANES * len(POOL_WINDOWS)
CONV_HALO = 32
MIX_TILE = 512
CONV_ROWS = 32
LN_BLOCK = 256
WEIGHT_SPLIT = 2
OUT_ROWS = 256
TAIL_ROWS = 128
PROJ_PIECE = 256
MOE_SUB = 1024
EXPERTS_PER_STEP = 4
EXPERT_TILE = 144
ROW_UNROLL = 16
VMEM_LIMIT = 56 * 1024 * 1024
MOE_VMEM_LIMIT = 60 * 1024 * 1024

F32 = jnp.float32
BF16 = jnp.bfloat16


def _silu(v):
    return v * jax.nn.sigmoid(v)


def _split_bf16(v):
    hi = v.astype(BF16)
    lo = (v - hi.astype(F32)).astype(BF16)
    return hi, lo


def _adaln_kernel(c_ref, w_ref, b_ref, o_ref):
    cond = _silu(c_ref[...])
    o_ref[...] = jnp.dot(cond, w_ref[...], preferred_element_type=F32,
                         precision=lax.Precision.HIGHEST) + b_ref[...]


def _adaln(c, ada_w, ada_b):
    depth, d, n = ada_w.shape
    b = c.shape[0]
    tn = 2 * D_MODEL
    return pl.pallas_call(
        _adaln_kernel,
        grid=(depth, n // tn),
        in_specs=[
            pl.BlockSpec((b, d), lambda l, j: (0, 0)),
            pl.BlockSpec((None, d, tn), lambda l, j: (l, 0, j)),
            pl.BlockSpec((None, 1, tn), lambda l, j: (l, 0, j)),
        ],
        out_specs=pl.BlockSpec((None, b, tn), lambda l, j: (l, 0, j)),
        out_shape=jax.ShapeDtypeStruct((depth, b, n), F32),
        compiler_params=pltpu.CompilerParams(dimension_semantics=("arbitrary", "arbitrary")),
        name="adaln",
    )(c, ada_w, ada_b.reshape(depth, 1, n))


def _route(logits, tril_ref, cnt_scr):
    t = logits.shape[0]
    lane = lax.broadcasted_iota(jnp.int32, (t, LANES), 1)
    lanef = lane.astype(F32)
    neg = float("-inf")
    big = float(4 * LANES)
    is_group = (lane >= N_EXPERTS) & (lane < N_EXPERTS + N_GROUPS)
    gl = jnp.where(is_group, logits, neg)
    gmax = jnp.max(gl, axis=-1, keepdims=True)
    gidx = jnp.min(jnp.where(gl == gmax, lanef, big), axis=-1, keepdims=True)
    p_group = 1.0 / jnp.sum(jnp.exp(gl - gmax), axis=-1, keepdims=True)
    gsel = gidx.astype(jnp.int32) - N_EXPERTS
    in_group = (lane < N_EXPERTS) & ((lane >> 3) == gsel)
    el = jnp.where(in_group, logits, neg)
    m1 = jnp.max(el, axis=-1, keepdims=True)
    i1 = jnp.min(jnp.where(el == m1, lanef, big), axis=-1, keepdims=True)
    el2 = jnp.where(lanef == i1, neg, el)
    m2 = jnp.max(el2, axis=-1, keepdims=True)
    i2 = jnp.min(jnp.where(el2 == m2, lanef, big), axis=-1, keepdims=True)
    ratio = jnp.exp(m2 - m1)
    w1 = p_group / (1.0 + ratio)
    w2 = p_group * ratio / (1.0 + ratio)
    hot1 = lanef == i1
    hot2 = lanef == i2
    onehot = jnp.where(hot1, 1.0, 0.0) + jnp.where(hot2, 1.0, 0.0)
    before = jnp.dot(tril_ref[...], onehot.astype(BF16), preferred_element_type=F32) + cnt_scr[...]
    rank1 = jnp.sum(jnp.where(hot1, before, 0.0), axis=-1, keepdims=True)
    rank2 = jnp.sum(jnp.where(hot2, before, 0.0), axis=-1, keepdims=True)
    cnt_scr[...] = cnt_scr[...] + jnp.sum(onehot, axis=0, keepdims=True)
    slab = jnp.zeros((t, LANES), F32)
    for col, val in enumerate((i1, i2, w1, w2, rank1, rank2)):
        slab = jnp.where(lane == col, val, slab)
    return slab


def _mixer_kernel(x_ref, xnext_ref, mod_ref, modnext_ref, n1g_ref, win_ref, poolw_ref, pscale_ref, convw_ref,
                  convb_ref, lng_ref, lnb_ref, avg_ref, wout_ref, n2g_ref, wr_ref, br_ref, tril_ref,
                  ewg_ref, ewu_ref, ewd_ref,
                  xo_ref, h2_ref, route_ref, cnt_ref, wgu_ref, wd_ref,
                  uext, pbuf, vsh, cbuf, ycat, hbuf, zbuf, cnt_scr):
    t = MIX_TILE
    i = pl.program_id(0)
    j = pl.program_id(1)

    wgu_ref[:, 0:EXPERT_HIDDEN] = ewg_ref[...].astype(BF16)
    wgu_ref[:, EXPERT_HIDDEN:] = ewu_ref[...].astype(BF16)
    wd_ref[...] = ewd_ref[...].astype(BF16)

    gate1 = mod_ref[2:3, :]
    shift2, scale2 = mod_ref[3:4, :], mod_ref[4:5, :]

    def norm_in(src_ref, m_ref):
        xin = src_ref[...]
        ms = jnp.mean(xin * xin, axis=-1, keepdims=True)
        gain = n1g_ref[...] * (1.0 + m_ref[1:2, :])
        hbuf[...] = (xin * lax.rsqrt(ms + RMS_EPS) * gain + m_ref[0:1, :]).astype(BF16)

    def project(piece):
        zbuf[piece] = jnp.dot(hbuf[...], win_ref[piece], preferred_element_type=F32)

    def projected(col):
        piece, off = divmod(col, PROJ_PIECE)
        return zbuf[piece, :, off:off + LANES]

    n_pieces = win_ref.shape[0]

    @pl.when((i == 0) & (j == 0))
    def _():
        norm_in(x_ref, mod_ref)
        for piece in range(n_pieces):
            project(piece)

    @pl.when(j == 0)
    def _():
        cnt_scr[...] = jnp.zeros_like(cnt_scr)
        uext[0:POOL_HALO, :] = jnp.zeros((POOL_HALO, POOL_WIDTH), F32)
        vsh[0, 0:CONV_HALO, :] = jnp.zeros((CONV_HALO, CONV_WIDTH), F32)

    @pl.when(j > 0)
    def _():
        uext[0:POOL_HALO, :] = uext[t:t + POOL_HALO, :]
        vsh[0, 0:CONV_HALO, :] = vsh[0, t:t + CONV_HALO, :]

    for p in range(POOL_WIDTH // LANES):
        uext[POOL_HALO:POOL_HALO + t, p * LANES:(p + 1) * LANES] = projected(p * LANES)
    for p in range(CONV_WIDTH // LANES):
        value = projected(POOL_WIDTH + p * LANES)
        gate = projected(POOL_WIDTH + CONV_WIDTH + p * LANES)
        vsh[0, CONV_HALO:CONV_HALO + t, p * LANES:(p + 1) * LANES] = value * jax.nn.sigmoid(gate)

    norm_in(xnext_ref, modnext_ref)
    pieces = iter(range(n_pieces))

    def project_some(n):
        for piece in itertools.islice(pieces, n):
            project(piece)

    pos = j * t + lax.broadcasted_iota(jnp.int32, (t, 1), 0)
    rows = POOL_HALO + t
    for g, w in enumerate(POOL_WINDOWS):
        assert w == 2 << g
        project_some(1)
        half, lo, c0 = w // 2, SUBLANES * (g + 1), g * POOL_GROUP_DIM
        if g == 0:
            level = uext[lo:rows, c0:] + uext[lo - half:rows - half, c0:]
        else:
            level = pbuf[g - 1, lo:rows, c0:] + pbuf[g - 1, lo - half:rows - half, c0:]
        if g + 1 < len(POOL_WINDOWS):
            pbuf[g, lo:rows, c0:] = level
        s = level[POOL_HALO - lo:, 0:POOL_GROUP_DIM]
        cur = uext[POOL_HALO:rows, c0:c0 + POOL_GROUP_DIM]
        inv_count = 1.0 / jnp.minimum(pos + 1, w).astype(F32)
        pooled = s * inv_count - cur
        yp = jnp.dot(pooled.astype(BF16), poolw_ref[g], preferred_element_type=F32)
        ycat[:, c0:c0 + POOL_GROUP_DIM] = (yp * pscale_ref[:, c0:c0 + POOL_GROUP_DIM]).astype(BF16)

    shifted_rows = CONV_HALO + t - SUBLANES
    for r in range(1, SUBLANES):
        if r % 3 == 1:
            project_some(1)
        vsh[r, 0:shifted_rows, :] = vsh[0, r:r + shifted_rows, :]
    project_some(n_pieces)

    def conv_chunk(c, carry):
        r0 = pl.multiple_of(c * CONV_ROWS, CONV_ROWS)
        groups = range(0, CONV_ROWS, SUBLANES)
        accs = [jnp.broadcast_to(convb_ref[...], (SUBLANES, CONV_WIDTH)) for _ in groups]
        for k in range(CONV_KERNEL):
            q, r = divmod(CONV_HALO - (CONV_KERNEL - 1) + k, SUBLANES)
            w = convw_ref[k]
            for n, g in enumerate(groups):
                accs[n] = accs[n] + w * vsh[r, pl.ds(r0 + q * SUBLANES + g, SUBLANES), :]
        for n, g in enumerate(groups):
            cbuf[pl.ds(r0 + g, SUBLANES), :] = accs[n]
        return carry

    lax.fori_loop(0, t // CONV_ROWS, conv_chunk, 0)

    blocks = [slice(r, r + TAIL_ROWS) for r in range(0, t, TAIL_ROWS)]

    for rs in blocks:
        for c in range(CONV_WIDTH // LN_BLOCK):
            c0 = c * LN_BLOCK
            yc = cbuf[rs, c0:c0 + LN_BLOCK]
            hi, lo = _split_bf16(yc)
            mu = (jnp.dot(hi, avg_ref[...], preferred_element_type=F32)
                  + jnp.dot(lo, avg_ref[...], preferred_element_type=F32))
            dlt = yc - mu
            hi, lo = _split_bf16(dlt * dlt)
            var = (jnp.dot(hi, avg_ref[...], preferred_element_type=F32)
                   + jnp.dot(lo, avg_ref[...], preferred_element_type=F32))
            yn = dlt * lax.rsqrt(var + LN_EPS) * lng_ref[:, c0:c0 + LN_BLOCK] + lnb_ref[:, c0:c0 + LN_BLOCK]
            ycat[rs, POOL_WIDTH + c0:POOL_WIDTH + c0 + LN_BLOCK] = _silu(yn).astype(BF16)

    x1s = []
    for r in range(0, t, OUT_ROWS):
        rs = slice(r, r + OUT_ROWS)
        x1 = x_ref[rs, :] + gate1 * jnp.dot(ycat[rs, :], wout_ref[...], preferred_element_type=F32)
        xo_ref[rs, :] = x1
        x1s.extend(x1[q:q + TAIL_ROWS] for q in range(0, OUT_ROWS, TAIL_ROWS))

    h2s = []
    for rs, x1 in zip(blocks, x1s):
        ms2 = jnp.mean(x1 * x1, axis=-1, keepdims=True)
        h2 = x1 * lax.rsqrt(ms2 + RMS_EPS) * (n2g_ref[...] * (1.0 + scale2)) + shift2
        words = _pack_row(h2)
        for s in range(PACK_ROWS):
            h2_ref[_token_chunk(rs.start, s, TAIL_ROWS), :] = words[:, s * LANES:(s + 1) * LANES]
        h2s.append(h2)

    logits = []
    for h2 in h2s:
        hi, lo = _split_bf16(h2)
        prod = (jnp.dot(hi, wr_ref[...], preferred_element_type=F32)
                + jnp.dot(lo, wr_ref[...], preferred_element_type=F32))
        logits.append(prod[:, 0:LANES] + prod[:, LANES:2 * LANES] + br_ref[...])

    for rs, lg in zip(blocks, logits):
        route_ref[:, rs] = _route(lg, tril_ref, cnt_scr).T[0:SUBLANES, :]
    cnt_ref[...] = cnt_scr[...]


def _mixer(x, mod, n1g, win, poolw, pscale, convw, convb, lng, lnb, avg, wout, n2g, wr, br, tril,
           layer, ewg, ewu, ewd):
    b, s, d = x.shape
    t = MIX_TILE
    n_exp, hid = ewg.shape[1], ewg.shape[3]
    assert b * (s // t) == WEIGHT_SPLIT * n_exp
    wshare = lambda i, j: ((i * (s // t) + j) // WEIGHT_SPLIT, (i * (s // t) + j) % WEIGHT_SPLIT, 0)
    assert d == SUBLANES * LANES and win.shape[1:] == (d, PROJ_PIECE)
    nj = s // t
    succ = lambda i, j: (jnp.minimum(i + (j + 1) // nj, b - 1), jnp.where(i + (j + 1) // nj < b, (j + 1) % nj, j))
    const = lambda shape: pl.BlockSpec(shape, lambda i, j: (0,) * len(shape))
    tok = lambda width: pl.BlockSpec((None, t, width), lambda i, j: (i, j, 0))
    return pl.pallas_call(
        _mixer_kernel,
        grid=(b, s // t),
        in_specs=[
            tok(d),
            pl.BlockSpec((None, t, d), lambda i, j: (*succ(i, j), 0)),
            pl.BlockSpec((None, N_MOD, d), lambda i, j: (i, 0, 0)),
            pl.BlockSpec((None, N_MOD, d), lambda i, j: (succ(i, j)[0], 0, 0)),
            const((1, d)),
            const(win.shape), const(poolw.shape), const((1, POOL_WIDTH)),
            const(convw.shape), const((1, CONV_WIDTH)), const((1, CONV_WIDTH)), const((1, CONV_WIDTH)),
            const(avg.shape), const(wout.shape), const((1, d)),
            const(wr.shape), const((1, LANES)), const(tril.shape),
            pl.BlockSpec((None, None, d // WEIGHT_SPLIT, hid), lambda i, j: (layer, *wshare(i, j))),
            pl.BlockSpec((None, None, d // WEIGHT_SPLIT, hid), lambda i, j: (layer, *wshare(i, j))),
            pl.BlockSpec((None, None, hid // WEIGHT_SPLIT, d), lambda i, j: (layer, *wshare(i, j))),
        ],
        out_specs=[
            tok(d),
            pl.BlockSpec((None, t * PACK_ROWS, LANES), lambda i, j: (i, j, 0)),
            pl.BlockSpec((None, SUBLANES, t), lambda i, j: (i, 0, j)),
            pl.BlockSpec((None, 1, LANES), lambda i, j: (i, 0, 0)),
            pl.BlockSpec((None, d // WEIGHT_SPLIT, 2 * hid), wshare),
            pl.BlockSpec((None, hid // WEIGHT_SPLIT, d), wshare),
        ],
        out_shape=[
            jax.ShapeDtypeStruct((b, s, d), F32),
            jax.ShapeDtypeStruct((b, s * PACK_ROWS, LANES), jnp.int32),
            jax.ShapeDtypeStruct((b, SUBLANES, s), F32),
            jax.ShapeDtypeStruct((b, 1, LANES), F32),
            jax.ShapeDtypeStruct((n_exp, d, 2 * hid), BF16),
            jax.ShapeDtypeStruct((n_exp, hid, d), BF16),
        ],
        scratch_shapes=[
            pltpu.VMEM((POOL_HALO + t, POOL_WIDTH), F32),
            pltpu.VMEM((len(POOL_WINDOWS) - 1, POOL_HALO + t, POOL_WIDTH), F32),
            pltpu.VMEM((SUBLANES, CONV_HALO + t, CONV_WIDTH), F32),
            pltpu.VMEM((t, CONV_WIDTH), F32),
            pltpu.VMEM((t, d), BF16),
            pltpu.VMEM((t, d), BF16),
            pltpu.VMEM((win.shape[0], t, PROJ_PIECE), F32),
            pltpu.VMEM((1, LANES), F32),
        ],
        compiler_params=pltpu.CompilerParams(
            dimension_semantics=("arbitrary", "arbitrary"), vmem_limit_bytes=VMEM_LIMIT),
        name="mixer",
    )(x, x, mod, mod, n1g, win, poolw, pscale, convw, convb, lng, lnb, avg, wout, n2g, wr, br, tril,
      ewg, ewu, ewd)


def _pack_row(v):
    half = v.shape[1] // 2
    return pltpu.pack_elementwise([v[:, :half], v[:, half:]], packed_dtype=BF16)


def _unpack_words(w, index):
    return pltpu.unpack_elementwise(w, index=index, packed_dtype=BF16, unpacked_dtype=F32)


def _token_rows(row0, rows=PACK_ROWS):
    return pl.ds(pl.multiple_of(row0, rows), rows)


def _token_chunk(tok0, s, n, rows=PACK_ROWS):
    return pl.ds(tok0 * rows + s, n, stride=rows)


def _moe_kernel(pos_ref, wts_ref, tbl_ref, h_ref, wgu_ref, wd_ref, x_ref, mod_ref, fg_ref, o_ref,
                gbuf, stage, *, final_norm, seq):
    nd = seq // MOE_SUB
    ne = N_EXPERTS // EXPERTS_PER_STEP
    b = pl.program_id(0)
    k = pl.program_id(1)
    base = b * (2 * seq)

    @pl.when(k == 0)
    def _():
        gbuf[pl.ds(2 * seq * PACK_ROWS, EXPERT_TILE * PACK_ROWS), :] = jnp.zeros(
            (EXPERT_TILE * PACK_ROWS, LANES), jnp.int32)

    @pl.when(k < nd)
    def _():
        t0 = k * MOE_SUB

        def body(i, carry):
            for u in range(ROW_UNROLL):
                t = i * ROW_UNROLL + u
                row = h_ref[_token_rows(t * PACK_ROWS), :]
                gbuf[_token_rows(pos_ref[base + t0 + t]), :] = row
                gbuf[_token_rows(pos_ref[base + seq + t0 + t]), :] = row
            return carry

        lax.fori_loop(0, MOE_SUB // ROW_UNROLL, body, 0)

    @pl.when((k >= nd) & (k < nd + ne))
    def _():
        starts, counts, ntiles = [], [], []
        for j in range(EXPERTS_PER_STEP):
            e = (k - nd) * EXPERTS_PER_STEP + j
            starts.append(tbl_ref[(b * N_EXPERTS + e) * 2])
            counts.append(tbl_ref[(b * N_EXPERTS + e) * 2 + 1])
            ntiles.append((counts[j] + EXPERT_TILE - 1) // EXPERT_TILE)

        def tile(i, carry):
            r0s = [jnp.where(i < ntiles[j], starts[j] + i * EXPERT_TILE, 2 * seq) for j in range(EXPERTS_PER_STEP)]
            xss = [[gbuf[_token_chunk(r0s[j], s, EXPERT_TILE), :] for s in range(PACK_ROWS)]
                   for j in range(EXPERTS_PER_STEP)]
            xbs = [jnp.concatenate([_unpack_words(w, 0) for w in xss[j]] + [_unpack_words(w, 1) for w in xss[j]],
                                   axis=-1).astype(BF16) for j in range(EXPERTS_PER_STEP)]
            acts = [jnp.dot(xbs[j], wgu_ref[j], preferred_element_type=F32) for j in range(EXPERTS_PER_STEP)]
            hids = [(_silu(a[:, :EXPERT_HIDDEN]) * a[:, EXPERT_HIDDEN:]).astype(BF16) for a in acts]
            ys = [jnp.dot(hids[j], wd_ref[j], preferred_element_type=F32) for j in range(EXPERTS_PER_STEP)]
            for j in range(EXPERTS_PER_STEP):
                valid = lax.broadcasted_iota(jnp.int32, (EXPERT_TILE, 1), 0) + i * EXPERT_TILE < counts[j]
                yw = _pack_row(ys[j])
                for s in range(PACK_ROWS):
                    gbuf[_token_chunk(r0s[j], s, EXPERT_TILE), :] = jnp.where(
                        valid, yw[:, s * LANES:(s + 1) * LANES], xss[j][s])
            return carry

        lax.fori_loop(0, functools.reduce(jnp.maximum, ntiles), tile, 0)

    @pl.when(k >= nd + ne)
    def _():
        t0 = (k - nd - ne) * MOE_SUB

        def body(i, carry):
            for u in range(ROW_UNROLL):
                t = i * ROW_UNROLL + u
                y1 = gbuf[_token_rows(pos_ref[base + t0 + t]), :]
                y2 = gbuf[_token_rows(pos_ref[base + seq + t0 + t]), :]
                w1 = wts_ref[base + t0 + t]
                w2 = wts_ref[base + seq + t0 + t]
                for half in range(2):
                    stage[pl.ds(pl.multiple_of(t * SUBLANES, SUBLANES) + half * PACK_ROWS, PACK_ROWS), :] = (
                        w1 * _unpack_words(y1, half) + w2 * _unpack_words(y2, half))
            return carry

        lax.fori_loop(0, MOE_SUB // ROW_UNROLL, body, 0)
        for s in range(SUBLANES):
            sl = slice(s * LANES, (s + 1) * LANES)
            o_ref[:, sl] = x_ref[:, sl] + mod_ref[5:6, sl] * stage[_token_chunk(0, s, MOE_SUB, SUBLANES), :]
        if final_norm:
            x2 = o_ref[...]
            ms = jnp.mean(x2 * x2, axis=-1, keepdims=True)
            o_ref[...] = x2 * lax.rsqrt(ms + RMS_EPS) * fg_ref[...]


def _moe(h2r, pos, wts, tbl, wgu, wd, x1, mod, final_g, final_norm):
    b, s, d = x1.shape
    nd = s // MOE_SUB
    ne = N_EXPERTS // EXPERTS_PER_STEP
    clip = lambda v, hi: jnp.minimum(jnp.maximum(v, 0), hi)
    tok_in = lambda i, k, *_: (i, clip(k, nd - 1), 0)
    tok_out = lambda i, k, *_: (i, clip(k - nd - ne, nd - 1), 0)
    expert = lambda i, k, *_: (clip(k - nd, ne - 1), 0, 0)
    grid_spec = pltpu.PrefetchScalarGridSpec(
        num_scalar_prefetch=3,
        grid=(b, nd + ne + nd),
        in_specs=[
            pl.BlockSpec((None, MOE_SUB * PACK_ROWS, LANES), tok_in),
            pl.BlockSpec((EXPERTS_PER_STEP, d, 2 * EXPERT_HIDDEN), expert),
            pl.BlockSpec((EXPERTS_PER_STEP, EXPERT_HIDDEN, d), expert),
            pl.BlockSpec((None, MOE_SUB, d), tok_out),
            pl.BlockSpec((None, N_MOD, d), lambda i, k, *_: (i, 0, 0)),
            pl.BlockSpec((1, d), lambda i, k, *_: (0, 0)),
        ],
        out_specs=pl.BlockSpec((None, MOE_SUB, d), tok_out),
        scratch_shapes=[
            pltpu.VMEM(((2 * s + EXPERT_TILE) * PACK_ROWS, LANES), jnp.int32),
            pltpu.VMEM((MOE_SUB * SUBLANES, LANES), F32),
        ],
    )
    return pl.pallas_call(
        functools.partial(_moe_kernel, final_norm=final_norm, seq=s),
        grid_spec=grid_spec,
        out_shape=jax.ShapeDtypeStruct((b, s, d), F32),
        compiler_params=pltpu.CompilerParams(
            dimension_semantics=("arbitrary", "arbitrary"), vmem_limit_bytes=MOE_VMEM_LIMIT),
        name="moe",
    )(pos, wts, tbl, h2r, wgu, wd, x1, mod, final_g)


def _dispatch_tables(route, counts):
    ids = route[:, 0:2, :].astype(jnp.int32)
    ranks = route[:, 4:6, :].astype(jnp.int32)
    cnt = counts[:, 0, :N_EXPERTS].astype(jnp.int32)
    starts = jnp.cumsum(cnt, axis=1) - cnt
    hot = ids[..., None] == jnp.arange(N_EXPERTS, dtype=jnp.int32)
    pos = jnp.sum(jnp.where(hot, starts[:, None, None, :], 0), axis=-1) + ranks
    row0 = (pos * PACK_ROWS).reshape(-1)
    wts = route[:, 2:4, :].reshape(-1)
    tbl = jnp.stack([starts, cnt], axis=-1).reshape(-1)
    return row0, wts, tbl


def _router_weights(rg_w, rg_b, re_w, re_b):
    d = rg_w.shape[0]
    pad = LANES - N_EXPERTS - N_GROUPS
    w = jnp.concatenate([re_w, rg_w, jnp.zeros((d, pad), F32)], axis=1)
    bias = jnp.concatenate([re_b, rg_b, jnp.zeros((pad,), F32)]).reshape(1, LANES)
    w_hi = w.astype(BF16)
    w_lo = (w - w_hi.astype(F32)).astype(BF16)
    return jnp.concatenate([w_hi, w_lo], axis=1), bias


def kernel(x, c, ada_w, ada_b, norm1_g, w_in, pool_w, pool_scale, conv_w, conv_b, conv_ln_g, conv_ln_b,
           w_out, norm2_g, router_group_w, router_group_b, router_expert_w, router_expert_b,
           expert_w_gate, expert_w_up, expert_w_down, final_g):
    depth = ada_w.shape[0]
    b = x.shape[0]
    mods = _adaln(c, ada_w, ada_b).reshape(depth, b, N_MOD, D_MODEL)
    head = lax.broadcasted_iota(jnp.int32, (LN_BLOCK, LN_BLOCK), 0) // CONV_HEAD_DIM
    head_t = lax.broadcasted_iota(jnp.int32, (LN_BLOCK, LN_BLOCK), 1) // CONV_HEAD_DIM
    avg = jnp.where(head == head_t, 1.0 / CONV_HEAD_DIM, 0.0).astype(BF16)
    earlier = (lax.broadcasted_iota(jnp.int32, (TAIL_ROWS, TAIL_ROWS), 1)
               < lax.broadcasted_iota(jnp.int32, (TAIL_ROWS, TAIL_ROWS), 0))
    tril = jnp.where(earlier, 1.0, 0.0).astype(BF16)
    row = lambda v: v.reshape(1, -1)
    for l in range(depth):
        win = jnp.swapaxes(w_in[l].astype(BF16).reshape(D_MODEL, -1, PROJ_PIECE), 0, 1)
        wr, br = _router_weights(router_group_w[l], router_group_b[l],
                                       router_expert_w[l], router_expert_b[l])
        x1, h2r, route, counts, wgu, wd = _mixer(
            x, mods[l], row(norm1_g[l]), win, pool_w[l].astype(BF16), row(pool_scale[l]),
            jnp.broadcast_to(conv_w[l][:, None, :], (CONV_KERNEL, SUBLANES, CONV_WIDTH)),
            row(conv_b[l]), row(conv_ln_g[l]), row(conv_ln_b[l]), avg,
            w_out[l].astype(BF16), row(norm2_g[l]), wr, br, tril,
            l, expert_w_gate, expert_w_up, expert_w_down)
        pos, wts, tbl = _dispatch_tables(route, counts)
        x = _moe(h2r, pos, wts, tbl, wgu, wd, x1, mods[l], row(final_g),
                 final_norm=(l == depth - 1))
    return x
```

```python
import functools
import itertools

import jax
import jax.numpy as jnp
from jax import lax
from jax.experimental import pallas as pl
from jax.experimental.pallas import tpu as pltpu

D_MODEL = 1024
POOL_WIDTH = 512
POOL_GROUPS = 4
POOL_GROUP_DIM = 128
POOL_WINDOWS = (2, 4, 8, 16)
CONV_WIDTH = 512
CONV_HEAD_DIM = 64
CONV_KERNEL = 31
N_GROUPS = 4
EXPERTS_PER_GROUP = 8
N_EXPERTS = 32
EXPERT_HIDDEN = 256
N_MOD = 6
RMS_EPS = 1e-6
LN_EPS = 1e-5

LANES = 128
SUBLANES = 8
PACK_ROWS = 4
POOL_HALO = SUBLANES * len(POOL_WINDOWS)
CONV_HALO = 32
MIX_TILE = 512
CONV_ROWS = 32
LN_BLOCK = 256
WEIGHT_SPLIT = 2
OUT_ROWS = 256
TAIL_ROWS = 128
PROJ_PIECE = 256
MOE_SEQS = 2
MOE_SUB = 512
EXPERTS_PER_STEP = 4
EXPERT_TILE = 144
ROW_UNROLL = 16
VMEM_LIMIT = 56 * 1024 * 1024
MOE_VMEM_LIMIT = 60 * 1024 * 1024

F32 = jnp.float32
BF16 = jnp.bfloat16


def _silu(v):
    return v * jax.nn.sigmoid(v)


def _split_bf16(v):
    hi = v.astype(BF16)
    lo = (v - hi.astype(F32)).astype(BF16)
    return hi, lo


def _adaln_kernel(c_ref, w_ref, b_ref, o_ref):
    cond = _silu(c_ref[...])
    o_ref[...] = jnp.dot(cond, w_ref[...], preferred_element_type=F32,
                         precision=lax.Precision.HIGHEST) + b_ref[...]


def _adaln(c, ada_w, ada_b):
    depth, d, n = ada_w.shape
    b = c.shape[0]
    tn = 2 * D_MODEL
    return pl.pallas_call(
        _adaln_kernel,
        grid=(depth, n // tn),
        in_specs=[
            pl.BlockSpec((b, d), lambda l, j: (0, 0)),
            pl.BlockSpec((None, d, tn), lambda l, j: (l, 0, j)),
            pl.BlockSpec((None, 1, tn), lambda l, j: (l, 0, j)),
        ],
        out_specs=pl.BlockSpec((None, b, tn), lambda l, j: (l, 0, j)),
        out_shape=jax.ShapeDtypeStruct((depth, b, n), F32),
        compiler_params=pltpu.CompilerParams(dimension_semantics=("arbitrary", "arbitrary")),
        name="adaln",
    )(c, ada_w, ada_b.reshape(depth, 1, n))


def _route(logits, tril_ref, cnt_scr):
    t = logits.shape[0]
    lane = lax.broadcasted_iota(jnp.int32, (t, LANES), 1)
    lanef = lane.astype(F32)
    neg = float("-inf")
    big = float(4 * LANES)
    is_group = (lane >= N_EXPERTS) & (lane < N_EXPERTS + N_GROUPS)
    gl = jnp.where(is_group, logits, neg)
    gmax = jnp.max(gl, axis=-1, keepdims=True)
    gidx = jnp.min(jnp.where(gl == gmax, lanef, big), axis=-1, keepdims=True)
    p_group = 1.0 / jnp.sum(jnp.exp(gl - gmax), axis=-1, keepdims=True)
    gsel = gidx.astype(jnp.int32) - N_EXPERTS
    in_group = (lane < N_EXPERTS) & ((lane >> 3) == gsel)
    el = jnp.where(in_group, logits, neg)
    m1 = jnp.max(el, axis=-1, keepdims=True)
    i1 = jnp.min(jnp.where(el == m1, lanef, big), axis=-1, keepdims=True)
    el2 = jnp.where(lanef == i1, neg, el)
    m2 = jnp.max(el2, axis=-1, keepdims=True)
    i2 = jnp.min(jnp.where(el2 == m2, lanef, big), axis=-1, keepdims=True)
    ratio = jnp.exp(m2 - m1)
    w1 = p_group / (1.0 + ratio)
    w2 = p_group * ratio / (1.0 + ratio)
    hot1 = lanef == i1
    hot2 = lanef == i2
    onehot = jnp.where(hot1, 1.0, 0.0) + jnp.where(hot2, 1.0, 0.0)
    before = jnp.dot(tril_ref[...], onehot.astype(BF16), preferred_element_type=F32) + cnt_scr[...]
    rank1 = jnp.sum(jnp.where(hot1, before, 0.0), axis=-1, keepdims=True)
    rank2 = jnp.sum(jnp.where(hot2, before, 0.0), axis=-1, keepdims=True)
    cnt_scr[...] = cnt_scr[...] + jnp.sum(onehot, axis=0, keepdims=True)
    slab = jnp.zeros((t, LANES), F32)
    for col, val in enumerate((i1, i2, w1, w2, rank1, rank2)):
        slab = jnp.where(lane == col, val, slab)
    return slab


def _mixer_kernel(x_ref, xnext_ref, mod_ref, modnext_ref, n1g_ref, win_ref, poolw_ref, pscale_ref, convw_ref,
                  convb_ref, lng_ref, lnb_ref, avg_ref, wout_ref, n2g_ref, wr_ref, br_ref, tril_ref,
                  ewg_ref, ewu_ref, ewd_ref,
                  xo_ref, h2_ref, route_ref, cnt_ref, wgu_ref, wd_ref,
                  uext, pbuf, vsh, cbuf, ycat, hbuf, zbuf, cnt_scr):
    t = MIX_TILE
    i = pl.program_id(0)
    j = pl.program_id(1)

    wgu_ref[:, 0:EXPERT_HIDDEN] = ewg_ref[...].astype(BF16)
    wgu_ref[:, EXPERT_HIDDEN:] = ewu_ref[...].astype(BF16)
    wd_ref[...] = ewd_ref[...].astype(BF16)

    gate1 = mod_ref[2:3, :]
    shift2, scale2 = mod_ref[3:4, :], mod_ref[4:5, :]

    def norm_in(src_ref, m_ref):
        xin = src_ref[...]
        ms = jnp.mean(xin * xin, axis=-1, keepdims=True)
        gain = n1g_ref[...] * (1.0 + m_ref[1:2, :])
        hbuf[...] = (xin * lax.rsqrt(ms + RMS_EPS) * gain + m_ref[0:1, :]).astype(BF16)

    def project(piece):
        zbuf[piece] = jnp.dot(hbuf[...], win_ref[piece], preferred_element_type=F32)

    def projected(col):
        piece, off = divmod(col, PROJ_PIECE)
        return zbuf[piece, :, off:off + LANES]

    n_pieces = win_ref.shape[0]

    @pl.when((i == 0) & (j == 0))
    def _():
        norm_in(x_ref, mod_ref)
        for piece in range(n_pieces):
            project(piece)

    @pl.when((j == 0) & (i % MOE_SEQS == 0))
    def _():
        cnt_scr[...] = jnp.zeros_like(cnt_scr)

    @pl.when(j == 0)
    def _():
        uext[0:POOL_HALO, :] = jnp.zeros((POOL_HALO, POOL_WIDTH), F32)
        vsh[0, 0:CONV_HALO, :] = jnp.zeros((CONV_HALO, CONV_WIDTH), F32)

    @pl.when(j > 0)
    def _():
        uext[0:POOL_HALO, :] = uext[t:t + POOL_HALO, :]
        vsh[0, 0:CONV_HALO, :] = vsh[0, t:t + CONV_HALO, :]

    for p in range(POOL_WIDTH // LANES):
        uext[POOL_HALO:POOL_HALO + t, p * LANES:(p + 1) * LANES] = projected(p * LANES)
    for p in range(CONV_WIDTH // LANES):
        value = projected(POOL_WIDTH + p * LANES)
        gate = projected(POOL_WIDTH + CONV_WIDTH + p * LANES)
        vsh[0, CONV_HALO:CONV_HALO + t, p * LANES:(p + 1) * LANES] = value * jax.nn.sigmoid(gate)

    norm_in(xnext_ref, modnext_ref)
    pieces = iter(range(n_pieces))

    def project_some(n):
        for piece in itertools.islice(pieces, n):
            project(piece)

    pos = j * t + lax.broadcasted_iota(jnp.int32, (t, 1), 0)
    rows = POOL_HALO + t
    for g, w in enumerate(POOL_WINDOWS):
        assert w == 2 << g
        project_some(1)
        half, lo, c0 = w // 2, SUBLANES * (g + 1), g * POOL_GROUP_DIM
        if g == 0:
            level = uext[lo:rows, c0:] + uext[lo - half:rows - half, c0:]
        else:
            level = pbuf[g - 1, lo:rows, c0:] + pbuf[g - 1, lo - half:rows - half, c0:]
        if g + 1 < len(POOL_WINDOWS):
            pbuf[g, lo:rows, c0:] = level
        s = level[POOL_HALO - lo:, 0:POOL_GROUP_DIM]
        cur = uext[POOL_HALO:rows, c0:c0 + POOL_GROUP_DIM]
        inv_count = 1.0 / jnp.minimum(pos + 1, w).astype(F32)
        pooled = s * inv_count - cur
        yp = jnp.dot(pooled.astype(BF16), poolw_ref[g], preferred_element_type=F32)
        ycat[:, c0:c0 + POOL_GROUP_DIM] = (yp * pscale_ref[:, c0:c0 + POOL_GROUP_DIM]).astype(BF16)

    shifted_rows = CONV_HALO + t - SUBLANES
    for r in range(1, SUBLANES):
        if r % 3 == 1:
            project_some(1)
        vsh[r, 0:shifted_rows, :] = vsh[0, r:r + shifted_rows, :]
    project_some(n_pieces)

    def conv_chunk(c, carry):
        r0 = pl.multiple_of(c * CONV_ROWS, CONV_ROWS)
        groups = range(0, CONV_ROWS, SUBLANES)
        accs = [jnp.broadcast_to(convb_ref[...], (SUBLANES, CONV_WIDTH)) for _ in groups]
        for k in range(CONV_KERNEL):
            q, r = divmod(CONV_HALO - (CONV_KERNEL - 1) + k, SUBLANES)
            w = convw_ref[k]
            for n, g in enumerate(groups):
                accs[n] = accs[n] + w * vsh[r, pl.ds(r0 + q * SUBLANES + g, SUBLANES), :]
        for n, g in enumerate(groups):
            cbuf[pl.ds(r0 + g, SUBLANES), :] = accs[n]
        return carry

    lax.fori_loop(0, t // CONV_ROWS, conv_chunk, 0)

    blocks = [slice(r, r + TAIL_ROWS) for r in range(0, t, TAIL_ROWS)]

    for rs in blocks:
        for c in range(CONV_WIDTH // LN_BLOCK):
            c0 = c * LN_BLOCK
            yc = cbuf[rs, c0:c0 + LN_BLOCK]
            hi, lo = _split_bf16(yc)
            mu = (jnp.dot(hi, avg_ref[...], preferred_element_type=F32)
                  + jnp.dot(lo, avg_ref[...], preferred_element_type=F32))
            dlt = yc - mu
            hi, lo = _split_bf16(dlt * dlt)
            var = (jnp.dot(hi, avg_ref[...], preferred_element_type=F32)
                   + jnp.dot(lo, avg_ref[...], preferred_element_type=F32))
            yn = dlt * lax.rsqrt(var + LN_EPS) * lng_ref[:, c0:c0 + LN_BLOCK] + lnb_ref[:, c0:c0 + LN_BLOCK]
            ycat[rs, POOL_WIDTH + c0:POOL_WIDTH + c0 + LN_BLOCK] = _silu(yn).astype(BF16)

    x1s = []
    for r in range(0, t, OUT_ROWS):
        rs = slice(r, r + OUT_ROWS)
        x1 = x_ref[rs, :] + gate1 * jnp.dot(ycat[rs, :], wout_ref[...], preferred_element_type=F32)
        xo_ref[rs, :] = x1
        x1s.extend(x1[q:q + TAIL_ROWS] for q in range(0, OUT_ROWS, TAIL_ROWS))

    h2s = []
    for rs, x1 in zip(blocks, x1s):
        ms2 = jnp.mean(x1 * x1, axis=-1, keepdims=True)
        h2 = x1 * lax.rsqrt(ms2 + RMS_EPS) * (n2g_ref[...] * (1.0 + scale2)) + shift2
        words = _pack_row(h2)
        for s in range(PACK_ROWS):
            h2_ref[_token_chunk(rs.start, s, TAIL_ROWS), :] = words[:, s * LANES:(s + 1) * LANES]
        h2s.append(h2)

    logits = []
    for h2 in h2s:
        hi, lo = _split_bf16(h2)
        prod = (jnp.dot(hi, wr_ref[...], preferred_element_type=F32)
                + jnp.dot(lo, wr_ref[...], preferred_element_type=F32))
        logits.append(prod[:, 0:LANES] + prod[:, LANES:2 * LANES] + br_ref[...])

    for rs, lg in zip(blocks, logits):
        route_ref[:, rs] = _route(lg, tril_ref, cnt_scr).T[0:SUBLANES, :]
    cnt_ref[...] = cnt_scr[...]


def _mixer(x, mod, n1g, win, poolw, pscale, convw, convb, lng, lnb, avg, wout, n2g, wr, br, tril,
           layer, ewg, ewu, ewd):
    b, s, d = x.shape
    t = MIX_TILE
    n_exp, hid = ewg.shape[1], ewg.shape[3]
    assert b * (s // t) == WEIGHT_SPLIT * n_exp
    wshare = lambda i, j: ((i * (s // t) + j) // WEIGHT_SPLIT, (i * (s // t) + j) % WEIGHT_SPLIT, 0)
    assert d == SUBLANES * LANES and win.shape[1:] == (d, PROJ_PIECE)
    nj = s // t
    succ = lambda i, j: (jnp.minimum(i + (j + 1) // nj, b - 1), jnp.where(i + (j + 1) // nj < b, (j + 1) % nj, j))
    const = lambda shape: pl.BlockSpec(shape, lambda i, j: (0,) * len(shape))
    tok = lambda width: pl.BlockSpec((None, t, width), lambda i, j: (i, j, 0))
    return pl.pallas_call(
        _mixer_kernel,
        grid=(b, s // t),
        in_specs=[
            tok(d),
            pl.BlockSpec((None, t, d), lambda i, j: (*succ(i, j), 0)),
            pl.BlockSpec((None, N_MOD, d), lambda i, j: (i, 0, 0)),
            pl.BlockSpec((None, N_MOD, d), lambda i, j: (succ(i, j)[0], 0, 0)),
            const((1, d)),
            const(win.shape), const(poolw.shape), const((1, POOL_WIDTH)),
            const(convw.shape), const((1, CONV_WIDTH)), const((1, CONV_WIDTH)), const((1, CONV_WIDTH)),
            const(avg.shape), const(wout.shape), const((1, d)),
            const(wr.shape), const((1, LANES)), const(tril.shape),
            pl.BlockSpec((None, None, d // WEIGHT_SPLIT, hid), lambda i, j: (layer, *wshare(i, j))),
            pl.BlockSpec((None, None, d // WEIGHT_SPLIT, hid), lambda i, j: (layer, *wshare(i, j))),
            pl.BlockSpec((None, None, hid // WEIGHT_SPLIT, d), lambda i, j: (layer, *wshare(i, j))),
        ],
        out_specs=[
            tok(d),
            pl.BlockSpec((None, t * PACK_ROWS, LANES), lambda i, j: (i, j, 0)),
            pl.BlockSpec((None, SUBLANES, t), lambda i, j: (i, 0, j)),
            pl.BlockSpec((None, 1, LANES), lambda i, j: (i // MOE_SEQS, 0, 0)),
            pl.BlockSpec((None, d // WEIGHT_SPLIT, 2 * hid), wshare),
            pl.BlockSpec((None, hid // WEIGHT_SPLIT, d), wshare),
        ],
        out_shape=[
            jax.ShapeDtypeStruct((b, s, d), F32),
            jax.ShapeDtypeStruct((b, s * PACK_ROWS, LANES), jnp.int32),
            jax.ShapeDtypeStruct((b, SUBLANES, s), F32),
            jax.ShapeDtypeStruct((b // MOE_SEQS, 1, LANES), F32),
            jax.ShapeDtypeStruct((n_exp, d, 2 * hid), BF16),
            jax.ShapeDtypeStruct((n_exp, hid, d), BF16),
        ],
        scratch_shapes=[
            pltpu.VMEM((POOL_HALO + t, POOL_WIDTH), F32),
            pltpu.VMEM((len(POOL_WINDOWS) - 1, POOL_HALO + t, POOL_WIDTH), F32),
            pltpu.VMEM((SUBLANES, CONV_HALO + t, CONV_WIDTH), F32),
            pltpu.VMEM((t, CONV_WIDTH), F32),
            pltpu.VMEM((t, d), BF16),
            pltpu.VMEM((t, d), BF16),
            pltpu.VMEM((win.shape[0], t, PROJ_PIECE), F32),
            pltpu.VMEM((1, LANES), F32),
        ],
        compiler_params=pltpu.CompilerParams(
            dimension_semantics=("arbitrary", "arbitrary"), vmem_limit_bytes=VMEM_LIMIT),
        name="mixer",
    )(x, x, mod, mod, n1g, win, poolw, pscale, convw, convb, lng, lnb, avg, wout, n2g, wr, br, tril,
      ewg, ewu, ewd)


def _pack_row(v):
    half = v.shape[1] // 2
    return pltpu.pack_elementwise([v[:, :half], v[:, half:]], packed_dtype=BF16)


def _unpack_words(w, index):
    return pltpu.unpack_elementwise(w, index=index, packed_dtype=BF16, unpacked_dtype=F32)


def _token_rows(row0, rows=PACK_ROWS):
    return pl.ds(pl.multiple_of(row0, rows), rows)


def _token_chunk(tok0, s, n, rows=PACK_ROWS):
    return pl.ds(tok0 * rows + s, n, stride=rows)


def _moe_kernel(pos_ref, wts_ref, tbl_ref, h_ref, wgu_ref, wd_ref, x_ref, mod_ref, fg_ref, o_ref,
                gbuf, stage, *, final_norm, seq):
    nd = seq // MOE_SUB
    ne = N_EXPERTS // EXPERTS_PER_STEP
    b = pl.program_id(0)
    k = pl.program_id(1)
    base = b * (2 * seq)

    @pl.when(k == 0)
    def _():
        gbuf[pl.ds(2 * seq * PACK_ROWS, EXPERT_TILE * PACK_ROWS), :] = jnp.zeros(
            (EXPERT_TILE * PACK_ROWS, LANES), jnp.int32)

    @pl.when(k < nd)
    def _():
        t0 = k * MOE_SUB

        def body(i, carry):
            for u in range(ROW_UNROLL):
                t = i * ROW_UNROLL + u
                row = h_ref[_token_rows(t * PACK_ROWS), :]
                gbuf[_token_rows(pos_ref[base + t0 + t]), :] = row
                gbuf[_token_rows(pos_ref[base + seq + t0 + t]), :] = row
            return carry

        lax.fori_loop(0, MOE_SUB // ROW_UNROLL, body, 0)

    @pl.when((k >= nd) & (k < nd + ne))
    def _():
        starts, counts, ntiles = [], [], []
        for j in range(EXPERTS_PER_STEP):
            e = (k - nd) * EXPERTS_PER_STEP + j
            starts.append(tbl_ref[(b * N_EXPERTS + e) * 2])
            counts.append(tbl_ref[(b * N_EXPERTS + e) * 2 + 1])
            ntiles.append((counts[j] + EXPERT_TILE - 1) // EXPERT_TILE)

        def tile(i, carry):
            r0s = [jnp.where(i < ntiles[j], starts[j] + i * EXPERT_TILE, 2 * seq) for j in range(EXPERTS_PER_STEP)]
            xss = [[gbuf[_token_chunk(r0s[j], s, EXPERT_TILE), :] for s in range(PACK_ROWS)]
                   for j in range(EXPERTS_PER_STEP)]
            xbs = [jnp.concatenate([_unpack_words(w, 0) for w in xss[j]] + [_unpack_words(w, 1) for w in xss[j]],
                                   axis=-1).astype(BF16) for j in range(EXPERTS_PER_STEP)]
            acts = [jnp.dot(xbs[j], wgu_ref[j], preferred_element_type=F32) for j in range(EXPERTS_PER_STEP)]
            hids = [(_silu(a[:, :EXPERT_HIDDEN]) * a[:, EXPERT_HIDDEN:]).astype(BF16) for a in acts]
            ys = [jnp.dot(hids[j], wd_ref[j], preferred_element_type=F32) for j in range(EXPERTS_PER_STEP)]
            for j in range(EXPERTS_PER_STEP):
                valid = lax.broadcasted_iota(jnp.int32, (EXPERT_TILE, 1), 0) + i * EXPERT_TILE < counts[j]
                yw = _pack_row(ys[j])
                for s in range(PACK_ROWS):
                    gbuf[_token_chunk(r0s[j], s, EXPERT_TILE), :] = jnp.where(
                        valid, yw[:, s * LANES:(s + 1) * LANES], xss[j][s])
            return carry

        lax.fori_loop(0, functools.reduce(jnp.maximum, ntiles), tile, 0)

    @pl.when(k >= nd + ne)
    def _():
        t0 = (k - nd - ne) * MOE_SUB

        def body(i, carry):
            for u in range(ROW_UNROLL):
                t = i * ROW_UNROLL + u
                y1 = gbuf[_token_rows(pos_ref[base + t0 + t]), :]
                y2 = gbuf[_token_rows(pos_ref[base + seq + t0 + t]), :]
                w1 = wts_ref[base + t0 + t]
                w2 = wts_ref[base + seq + t0 + t]
                for half in range(2):
                    stage[pl.ds(pl.multiple_of(t * SUBLANES, SUBLANES) + half * PACK_ROWS, PACK_ROWS), :] = (
                        w1 * _unpack_words(y1, half) + w2 * _unpack_words(y2, half))
            return carry

        lax.fori_loop(0, MOE_SUB // ROW_UNROLL, body, 0)
        for s in range(SUBLANES):
            sl = slice(s * LANES, (s + 1) * LANES)
            o_ref[:, sl] = x_ref[:, sl] + mod_ref[5:6, sl] * stage[_token_chunk(0, s, MOE_SUB, SUBLANES), :]
        if final_norm:
            x2 = o_ref[...]
            ms = jnp.mean(x2 * x2, axis=-1, keepdims=True)
            o_ref[...] = x2 * lax.rsqrt(ms + RMS_EPS) * fg_ref[...]


def _moe(h2r, pos, wts, tbl, wgu, wd, x1, mod, final_g, final_norm):
    batch, seq_len, d = x1.shape
    b, s = batch // MOE_SEQS, seq_len * MOE_SEQS
    h2r = h2r.reshape(b, s * PACK_ROWS, LANES)
    x1 = x1.reshape(b, s, d)
    per_seq = seq_len // MOE_SUB
    nd = s // MOE_SUB
    ne = N_EXPERTS // EXPERTS_PER_STEP
    clip = lambda v, hi: jnp.minimum(jnp.maximum(v, 0), hi)
    tok_in = lambda i, k, *_: (i, clip(k, nd - 1), 0)
    tok_out = lambda i, k, *_: (i, clip(k - nd - ne, nd - 1), 0)
    expert = lambda i, k, *_: (clip(k - nd, ne - 1), 0, 0)
    grid_spec = pltpu.PrefetchScalarGridSpec(
        num_scalar_prefetch=3,
        grid=(b, nd + ne + nd),
        in_specs=[
            pl.BlockSpec((None, MOE_SUB * PACK_ROWS, LANES), tok_in),
            pl.BlockSpec((EXPERTS_PER_STEP, d, 2 * EXPERT_HIDDEN), expert),
            pl.BlockSpec((EXPERTS_PER_STEP, EXPERT_HIDDEN, d), expert),
            pl.BlockSpec((None, MOE_SUB, d), tok_out),
            pl.BlockSpec((None, N_MOD, d),
                         lambda i, k, *_: (i * MOE_SEQS + clip(k - nd - ne, nd - 1) // per_seq, 0, 0)),
            pl.BlockSpec((1, d), lambda i, k, *_: (0, 0)),
        ],
        out_specs=pl.BlockSpec((None, MOE_SUB, d), tok_out),
        scratch_shapes=[
            pltpu.VMEM(((2 * s + EXPERT_TILE) * PACK_ROWS, LANES), jnp.int32),
            pltpu.VMEM((MOE_SUB * SUBLANES, LANES), F32),
        ],
    )
    return pl.pallas_call(
        functools.partial(_moe_kernel, final_norm=final_norm, seq=s),
        grid_spec=grid_spec,
        out_shape=jax.ShapeDtypeStruct((b, s, d), F32),
        compiler_params=pltpu.CompilerParams(
            dimension_semantics=("arbitrary", "arbitrary"), vmem_limit_bytes=MOE_VMEM_LIMIT),
        name="moe",
    )(pos, wts, tbl, h2r, wgu, wd, x1, mod, final_g).reshape(batch, seq_len, d)


def _dispatch_tables(route, counts):
    batch, ncol, seq_len = route.shape
    route = jnp.swapaxes(route.reshape(batch // MOE_SEQS, MOE_SEQS, ncol, seq_len), 1, 2).reshape(
        batch // MOE_SEQS, ncol, MOE_SEQS * seq_len)
    ids = route[:, 0:2, :].astype(jnp.int32)
    ranks = route[:, 4:6, :].astype(jnp.int32)
    cnt = counts[:, 0, :N_EXPERTS].astype(jnp.int32)
    starts = jnp.cumsum(cnt, axis=1) - cnt
    hot = ids[..., None] == jnp.arange(N_EXPERTS, dtype=jnp.int32)
    pos = jnp.sum(jnp.where(hot, starts[:, None, None, :], 0), axis=-1) + ranks
    row0 = (pos * PACK_ROWS).reshape(-1)
    wts = route[:, 2:4, :].reshape(-1)
    tbl = jnp.stack([starts, cnt], axis=-1).reshape(-1)
    return row0, wts, tbl


def _router_weights(rg_w, rg_b, re_w, re_b):
    d = rg_w.shape[0]
    pad = LANES - N_EXPERTS - N_GROUPS
    w = jnp.concatenate([re_w, rg_w, jnp.zeros((d, pad), F32)], axis=1)
    bias = jnp.concatenate([re_b, rg_b, jnp.zeros((pad,), F32)]).reshape(1, LANES)
    w_hi = w.astype(BF16)
    w_lo = (w - w_hi.astype(F32)).astype(BF16)
    return jnp.concatenate([w_hi, w_lo], axis=1), bias


def kernel(x, c, ada_w, ada_b, norm1_g, w_in, pool_w, pool_scale, conv_w, conv_b, conv_ln_g, conv_ln_b,
           w_out, norm2_g, router_group_w, router_group_b, router_expert_w, router_expert_b,
           expert_w_gate, expert_w_up, expert_w_down, final_g):
    depth = ada_w.shape[0]
    b = x.shape[0]
    mods = _adaln(c, ada_w, ada_b).reshape(depth, b, N_MOD, D_MODEL)
    head = lax.broadcasted_iota(jnp.int32, (LN_BLOCK, LN_BLOCK), 0) // CONV_HEAD_DIM
    head_t = lax.broadcasted_iota(jnp.int32, (LN_BLOCK, LN_BLOCK), 1) // CONV_HEAD_DIM
    avg = jnp.where(head == head_t, 1.0 / CONV_HEAD_DIM, 0.0).astype(BF16)
    earlier = (lax.broadcasted_iota(jnp.int32, (TAIL_ROWS, TAIL_ROWS), 1)
               < lax.broadcasted_iota(jnp.int32, (TAIL_ROWS, TAIL_ROWS), 0))
    tril = jnp.where(earlier, 1.0, 0.0).astype(BF16)
    row = lambda v: v.reshape(1, -1)
    for l in range(depth):
        win = jnp.swapaxes(w_in[l].astype(BF16).reshape(D_MODEL, -1, PROJ_PIECE), 0, 1)
        wr, br = _router_weights(router_group_w[l], router_group_b[l],
                                       router_expert_w[l], router_expert_b[l])
        x1, h2r, route, counts, wgu, wd = _mixer(
            x, mods[l], row(norm1_g[l]), win, pool_w[l].astype(BF16), row(pool_scale[l]),
            jnp.broadcast_to(conv_w[l][:, None, :], (CONV_KERNEL, SUBLANES, CONV_WIDTH)),
            row(conv_b[l]), row(conv_ln_g[l]), row(conv_ln_b[l]), avg,
            w_out[l].astype(BF16), row(norm2_g[l]), wr, br, tril,
            l, expert_w_gate, expert_w_up, expert_w_down)
        pos, wts, tbl = _dispatch_tables(route, counts)
        x = _moe(h2r, pos, wts, tbl, wgu, wd, x1, mods[l], row(final_g),
                 final_norm=(l == depth - 1))
    return x
```

```python
import functools
import itertools

import jax
import jax.numpy as jnp
from jax import lax
from jax.experimental import pallas as pl
from jax.experimental.pallas import tpu as pltpu

D_MODEL = 1024
POOL_WIDTH = 512
POOL_GROUPS = 4
POOL_GROUP_DIM = 128
POOL_WINDOWS = (2, 4, 8, 16)
CONV_WIDTH = 512
CONV_HEAD_DIM = 64
CONV_KERNEL = 31
N_GROUPS = 4
EXPERTS_PER_GROUP = 8
N_EXPERTS = 32
EXPERT_HIDDEN = 256
N_MOD = 6
RMS_EPS = 1e-6
LN_EPS = 1e-5

LANES = 128
SUBLANES = 8
PACK_ROWS = 4
POOL_HALO = SUBLANES * len(POOL_WINDOWS)
CONV_HALO = 32
MIX_TILE = 512
CONV_ROWS = 32
LN_BLOCK = 256
WEIGHT_SPLIT = 2
OUT_ROWS = 256
TAIL_ROWS = 128
PROJ_PIECE = 256
MOE_SEQS = 2
MOE_SUB = 512
EXPERTS_PER_STEP = 4
EXPERT_TILE = 144
ROW_UNROLL = 64
VMEM_LIMIT = 56 * 1024 * 1024
MOE_VMEM_LIMIT = 60 * 1024 * 1024

F32 = jnp.float32
BF16 = jnp.bfloat16


def _silu(v):
    return v * jax.nn.sigmoid(v)


def _split_bf16(v):
    hi = v.astype(BF16)
    lo = (v - hi.astype(F32)).astype(BF16)
    return hi, lo


def _adaln_kernel(c_ref, w_ref, b_ref, o_ref):
    cond = _silu(c_ref[...])
    o_ref[...] = jnp.dot(cond, w_ref[...], preferred_element_type=F32,
                         precision=lax.Precision.HIGHEST) + b_ref[...]


def _adaln(c, ada_w, ada_b):
    depth, d, n = ada_w.shape
    b = c.shape[0]
    tn = 2 * D_MODEL
    return pl.pallas_call(
        _adaln_kernel,
        grid=(depth, n // tn),
        in_specs=[
            pl.BlockSpec((b, d), lambda l, j: (0, 0)),
            pl.BlockSpec((None, d, tn), lambda l, j: (l, 0, j)),
            pl.BlockSpec((None, 1, tn), lambda l, j: (l, 0, j)),
        ],
        out_specs=pl.BlockSpec((None, b, tn), lambda l, j: (l, 0, j)),
        out_shape=jax.ShapeDtypeStruct((depth, b, n), F32),
        compiler_params=pltpu.CompilerParams(dimension_semantics=("arbitrary", "arbitrary")),
        name="adaln",
    )(c, ada_w, ada_b.reshape(depth, 1, n))


def _route(logits, tril_ref, cnt_scr):
    t = logits.shape[0]
    lane = lax.broadcasted_iota(jnp.int32, (t, LANES), 1)
    lanef = lane.astype(F32)
    neg = float("-inf")
    big = float(4 * LANES)
    is_group = (lane >= N_EXPERTS) & (lane < N_EXPERTS + N_GROUPS)
    gl = jnp.where(is_group, logits, neg)
    gmax = jnp.max(gl, axis=-1, keepdims=True)
    gidx = jnp.min(jnp.where(gl == gmax, lanef, big), axis=-1, keepdims=True)
    p_group = 1.0 / jnp.sum(jnp.exp(gl - gmax), axis=-1, keepdims=True)
    gsel = gidx.astype(jnp.int32) - N_EXPERTS
    in_group = (lane < N_EXPERTS) & ((lane >> 3) == gsel)
    el = jnp.where(in_group, logits, neg)
    m1 = jnp.max(el, axis=-1, keepdims=True)
    i1 = jnp.min(jnp.where(el == m1, lanef, big), axis=-1, keepdims=True)
    el2 = jnp.where(lanef == i1, neg, el)
    m2 = jnp.max(el2, axis=-1, keepdims=True)
    i2 = jnp.min(jnp.where(el2 == m2, lanef, big), axis=-1, keepdims=True)
    ratio = jnp.exp(m2 - m1)
    w1 = p_group / (1.0 + ratio)
    w2 = p_group * ratio / (1.0 + ratio)
    hot1 = lanef == i1
    hot2 = lanef == i2
    onehot = jnp.where(hot1, 1.0, 0.0) + jnp.where(hot2, 1.0, 0.0)
    before = jnp.dot(tril_ref[...], onehot.astype(BF16), preferred_element_type=F32) + cnt_scr[...]
    rank1 = jnp.sum(jnp.where(hot1, before, 0.0), axis=-1, keepdims=True)
    rank2 = jnp.sum(jnp.where(hot2, before, 0.0), axis=-1, keepdims=True)
    cnt_scr[...] = cnt_scr[...] + jnp.sum(onehot, axis=0, keepdims=True)
    slab = jnp.zeros((t, LANES), F32)
    for col, val in enumerate((i1, i2, w1, w2, rank1, rank2)):
        slab = jnp.where(lane == col, val, slab)
    return slab


def _mixer_kernel(x_ref, xnext_ref, mod_ref, modnext_ref, n1g_ref, win_ref, poolw_ref, pscale_ref, convw_ref,
                  convb_ref, lng_ref, lnb_ref, avg_ref, wout_ref, n2g_ref, wr_ref, br_ref, tril_ref,
                  ewg_ref, ewu_ref, ewd_ref,
                  xo_ref, h2_ref, route_ref, cnt_ref, wgu_ref, wd_ref,
                  uext, pbuf, vsh, cbuf, ycat, hbuf, zbuf, cnt_scr):
    t = MIX_TILE
    i = pl.program_id(0)
    j = pl.program_id(1)

    wgu_ref[:, 0:EXPERT_HIDDEN] = ewg_ref[...].astype(BF16)
    wgu_ref[:, EXPERT_HIDDEN:] = ewu_ref[...].astype(BF16)
    wd_ref[...] = ewd_ref[...].astype(BF16)

    gate1 = mod_ref[2:3, :]
    shift2, scale2 = mod_ref[3:4, :], mod_ref[4:5, :]

    def norm_in(src_ref, m_ref):
        xin = src_ref[...]
        ms = jnp.mean(xin * xin, axis=-1, keepdims=True)
        gain = n1g_ref[...] * (1.0 + m_ref[1:2, :])
        hbuf[...] = (xin * lax.rsqrt(ms + RMS_EPS) * gain + m_ref[0:1, :]).astype(BF16)

    def project(piece):
        zbuf[piece] = jnp.dot(hbuf[...], win_ref[piece], preferred_element_type=F32)

    def projected(col):
        piece, off = divmod(col, PROJ_PIECE)
        return zbuf[piece, :, off:off + LANES]

    n_pieces = win_ref.shape[0]

    @pl.when((i == 0) & (j == 0))
    def _():
        norm_in(x_ref, mod_ref)
        for piece in range(n_pieces):
            project(piece)

    @pl.when((j == 0) & (i % MOE_SEQS == 0))
    def _():
        cnt_scr[...] = jnp.zeros_like(cnt_scr)

    @pl.when(j == 0)
    def _():
        uext[0:POOL_HALO, :] = jnp.zeros((POOL_HALO, POOL_WIDTH), F32)
        vsh[0, 0:CONV_HALO, :] = jnp.zeros((CONV_HALO, CONV_WIDTH), F32)

    @pl.when(j > 0)
    def _():
        uext[0:POOL_HALO, :] = uext[t:t + POOL_HALO, :]
        vsh[0, 0:CONV_HALO, :] = vsh[0, t:t + CONV_HALO, :]

    for p in range(POOL_WIDTH // LANES):
        uext[POOL_HALO:POOL_HALO + t, p * LANES:(p + 1) * LANES] = projected(p * LANES)
    for p in range(CONV_WIDTH // LANES):
        value = projected(POOL_WIDTH + p * LANES)
        gate = projected(POOL_WIDTH + CONV_WIDTH + p * LANES)
        vsh[0, CONV_HALO:CONV_HALO + t, p * LANES:(p + 1) * LANES] = value * jax.nn.sigmoid(gate)

    norm_in(xnext_ref, modnext_ref)
    pieces = iter(range(n_pieces))

    def project_some(n):
        for piece in itertools.islice(pieces, n):
            project(piece)

    pos = j * t + lax.broadcasted_iota(jnp.int32, (t, 1), 0)
    rows = POOL_HALO + t
    for g, w in enumerate(POOL_WINDOWS):
        assert w == 2 << g
        project_some(1)
        half, lo, c0 = w // 2, SUBLANES * (g + 1), g * POOL_GROUP_DIM
        if g == 0:
            level = uext[lo:rows, c0:] + uext[lo - half:rows - half, c0:]
        else:
            level = pbuf[g - 1, lo:rows, c0:] + pbuf[g - 1, lo - half:rows - half, c0:]
        if g + 1 < len(POOL_WINDOWS):
            pbuf[g, lo:rows, c0:] = level
        s = level[POOL_HALO - lo:, 0:POOL_GROUP_DIM]
        cur = uext[POOL_HALO:rows, c0:c0 + POOL_GROUP_DIM]
        inv_count = 1.0 / jnp.minimum(pos + 1, w).astype(F32)
        pooled = s * inv_count - cur
        yp = jnp.dot(pooled.astype(BF16), poolw_ref[g], preferred_element_type=F32)
        ycat[:, c0:c0 + POOL_GROUP_DIM] = (yp * pscale_ref[:, c0:c0 + POOL_GROUP_DIM]).astype(BF16)

    shifted_rows = CONV_HALO + t - SUBLANES
    for r in range(1, SUBLANES):
        if r % 3 == 1:
            project_some(1)
        vsh[r, 0:shifted_rows, :] = vsh[0, r:r + shifted_rows, :]
    project_some(n_pieces)

    def conv_chunk(c, carry):
        r0 = pl.multiple_of(c * CONV_ROWS, CONV_ROWS)
        groups = range(0, CONV_ROWS, SUBLANES)
        accs = [jnp.broadcast_to(convb_ref[...], (SUBLANES, CONV_WIDTH)) for _ in groups]
        for k in range(CONV_KERNEL):
            q, r = divmod(CONV_HALO - (CONV_KERNEL - 1) + k, SUBLANES)
            w = convw_ref[k]
            for n, g in enumerate(groups):
                accs[n] = accs[n] + w * vsh[r, pl.ds(r0 + q * SUBLANES + g, SUBLANES), :]
        for n, g in enumerate(groups):
            cbuf[pl.ds(r0 + g, SUBLANES), :] = accs[n]
        return carry

    lax.fori_loop(0, t // CONV_ROWS, conv_chunk, 0)

    blocks = [slice(r, r + TAIL_ROWS) for r in range(0, t, TAIL_ROWS)]

    for rs in blocks:
        for c in range(CONV_WIDTH // LN_BLOCK):
            c0 = c * LN_BLOCK
            yc = cbuf[rs, c0:c0 + LN_BLOCK]
            hi, lo = _split_bf16(yc)
            mu = (jnp.dot(hi, avg_ref[...], preferred_element_type=F32)
                  + jnp.dot(lo, avg_ref[...], preferred_element_type=F32))
            dlt = yc - mu
            hi, lo = _split_bf16(dlt * dlt)
            var = (jnp.dot(hi, avg_ref[...], preferred_element_type=F32)
                   + jnp.dot(lo, avg_ref[...], preferred_element_type=F32))
            yn = dlt * lax.rsqrt(var + LN_EPS) * lng_ref[:, c0:c0 + LN_BLOCK] + lnb_ref[:, c0:c0 + LN_BLOCK]
            ycat[rs, POOL_WIDTH + c0:POOL_WIDTH + c0 + LN_BLOCK] = _silu(yn).astype(BF16)

    x1s = []
    for r in range(0, t, OUT_ROWS):
        rs = slice(r, r + OUT_ROWS)
        x1 = x_ref[rs, :] + gate1 * jnp.dot(ycat[rs, :], wout_ref[...], preferred_element_type=F32)
        xo_ref[rs, :] = x1
        x1s.extend(x1[q:q + TAIL_ROWS] for q in range(0, OUT_ROWS, TAIL_ROWS))

    h2s = []
    for rs, x1 in zip(blocks, x1s):
        ms2 = jnp.mean(x1 * x1, axis=-1, keepdims=True)
        h2 = x1 * lax.rsqrt(ms2 + RMS_EPS) * (n2g_ref[...] * (1.0 + scale2)) + shift2
        words = _pack_row(h2)
        for s in range(PACK_ROWS):
            h2_ref[_token_chunk(rs.start, s, TAIL_ROWS), :] = words[:, s * LANES:(s + 1) * LANES]
        h2s.append(h2)

    logits = []
    for h2 in h2s:
        hi, lo = _split_bf16(h2)
        prod = (jnp.dot(hi, wr_ref[...], preferred_element_type=F32)
                + jnp.dot(lo, wr_ref[...], preferred_element_type=F32))
        logits.append(prod[:, 0:LANES] + prod[:, LANES:2 * LANES] + br_ref[...])

    for rs, lg in zip(blocks, logits):
        route_ref[:, rs] = _route(lg, tril_ref, cnt_scr).T[0:SUBLANES, :]
    cnt_ref[...] = cnt_scr[...]


def _mixer(x, mod, n1g, win, poolw, pscale, convw, convb, lng, lnb, avg, wout, n2g, wr, br, tril,
           layer, ewg, ewu, ewd):
    b, s, d = x.shape
    t = MIX_TILE
    n_exp, hid = ewg.shape[1], ewg.shape[3]
    assert b * (s // t) == WEIGHT_SPLIT * n_exp
    wshare = lambda i, j: ((i * (s // t) + j) // WEIGHT_SPLIT, (i * (s // t) + j) % WEIGHT_SPLIT, 0)
    assert d == SUBLANES * LANES and win.shape[1:] == (d, PROJ_PIECE)
    nj = s // t
    succ = lambda i, j: (jnp.minimum(i + (j + 1) // nj, b - 1), jnp.where(i + (j + 1) // nj < b, (j + 1) % nj, j))
    const = lambda shape: pl.BlockSpec(shape, lambda i, j: (0,) * len(shape))
    tok = lambda width: pl.BlockSpec((None, t, width), lambda i, j: (i, j, 0))
    return pl.pallas_call(
        _mixer_kernel,
        grid=(b, s // t),
        in_specs=[
            tok(d),
            pl.BlockSpec((None, t, d), lambda i, j: (*succ(i, j), 0)),
            pl.BlockSpec((None, N_MOD, d), lambda i, j: (i, 0, 0)),
            pl.BlockSpec((None, N_MOD, d), lambda i, j: (succ(i, j)[0], 0, 0)),
            const((1, d)),
            const(win.shape), const(poolw.shape), const((1, POOL_WIDTH)),
            const(convw.shape), const((1, CONV_WIDTH)), const((1, CONV_WIDTH)), const((1, CONV_WIDTH)),
            const(avg.shape), const(wout.shape), const((1, d)),
            const(wr.shape), const((1, LANES)), const(tril.shape),
            pl.BlockSpec((None, None, d // WEIGHT_SPLIT, hid), lambda i, j: (layer, *wshare(i, j))),
            pl.BlockSpec((None, None, d // WEIGHT_SPLIT, hid), lambda i, j: (layer, *wshare(i, j))),
            pl.BlockSpec((None, None, hid // WEIGHT_SPLIT, d), lambda i, j: (layer, *wshare(i, j))),
        ],
        out_specs=[
            tok(d),
            pl.BlockSpec((None, t * PACK_ROWS, LANES), lambda i, j: (i, j, 0)),
            pl.BlockSpec((None, SUBLANES, t), lambda i, j: (i, 0, j)),
            pl.BlockSpec((None, 1, LANES), lambda i, j: (i // MOE_SEQS, 0, 0)),
            pl.BlockSpec((None, d // WEIGHT_SPLIT, 2 * hid), wshare),
            pl.BlockSpec((None, hid // WEIGHT_SPLIT, d), wshare),
        ],
        out_shape=[
            jax.ShapeDtypeStruct((b, s, d), F32),
            jax.ShapeDtypeStruct((b, s * PACK_ROWS, LANES), jnp.int32),
            jax.ShapeDtypeStruct((b, SUBLANES, s), F32),
            jax.ShapeDtypeStruct((b // MOE_SEQS, 1, LANES), F32),
            jax.ShapeDtypeStruct((n_exp, d, 2 * hid), BF16),
            jax.ShapeDtypeStruct((n_exp, hid, d), BF16),
        ],
        scratch_shapes=[
            pltpu.VMEM((POOL_HALO + t, POOL_WIDTH), F32),
            pltpu.VMEM((len(POOL_WINDOWS) - 1, POOL_HALO + t, POOL_WIDTH), F32),
            pltpu.VMEM((SUBLANES, CONV_HALO + t, CONV_WIDTH), F32),
            pltpu.VMEM((t, CONV_WIDTH), F32),
            pltpu.VMEM((t, d), BF16),
            pltpu.VMEM((t, d), BF16),
            pltpu.VMEM((win.shape[0], t, PROJ_PIECE), F32),
            pltpu.VMEM((1, LANES), F32),
        ],
        compiler_params=pltpu.CompilerParams(
            dimension_semantics=("arbitrary", "arbitrary"), vmem_limit_bytes=VMEM_LIMIT),
        name="mixer",
    )(x, x, mod, mod, n1g, win, poolw, pscale, convw, convb, lng, lnb, avg, wout, n2g, wr, br, tril,
      ewg, ewu, ewd)


def _pack_row(v):
    half = v.shape[1] // 2
    return pltpu.pack_elementwise([v[:, :half], v[:, half:]], packed_dtype=BF16)


def _unpack_words(w, index):
    return pltpu.unpack_elementwise(w, index=index, packed_dtype=BF16, unpacked_dtype=F32)


def _token_rows(row0, rows=PACK_ROWS):
    return pl.ds(pl.multiple_of(row0, rows), rows)


def _token_chunk(tok0, s, n, rows=PACK_ROWS):
    return pl.ds(tok0 * rows + s, n, stride=rows)


def _moe_kernel(pos_ref, wts_ref, tbl_ref, h_ref, wgu_ref, wd_ref, x_ref, mod_ref, fg_ref, o_ref,
                gbuf, stage, *, final_norm, seq):
    nd = seq // MOE_SUB
    ne = N_EXPERTS // EXPERTS_PER_STEP
    b = pl.program_id(0)
    k = pl.program_id(1)
    base = b * (2 * seq)

    @pl.when(k == 0)
    def _():
        gbuf[pl.ds(2 * seq * PACK_ROWS, EXPERT_TILE * PACK_ROWS), :] = jnp.zeros(
            (EXPERT_TILE * PACK_ROWS, LANES), jnp.int32)

    @pl.when(k < nd)
    def _():
        t0 = k * MOE_SUB

        def body(i, carry):
            for u in range(ROW_UNROLL):
                t = i * ROW_UNROLL + u
                row = h_ref[_token_rows(t * PACK_ROWS), :]
                gbuf[_token_rows(pos_ref[base + t0 + t]), :] = row
                gbuf[_token_rows(pos_ref[base + seq + t0 + t]), :] = row
            return carry

        lax.fori_loop(0, MOE_SUB // ROW_UNROLL, body, 0)

    @pl.when((k >= nd) & (k < nd + ne))
    def _():
        starts, counts, ntiles = [], [], []
        for j in range(EXPERTS_PER_STEP):
            e = (k - nd) * EXPERTS_PER_STEP + j
            starts.append(tbl_ref[(b * N_EXPERTS + e) * 2])
            counts.append(tbl_ref[(b * N_EXPERTS + e) * 2 + 1])
            ntiles.append((counts[j] + EXPERT_TILE - 1) // EXPERT_TILE)

        def tile(i, carry):
            r0s = [jnp.where(i < ntiles[j], starts[j] + i * EXPERT_TILE, 2 * seq) for j in range(EXPERTS_PER_STEP)]
            xss = [[gbuf[_token_chunk(r0s[j], s, EXPERT_TILE), :] for s in range(PACK_ROWS)]
                   for j in range(EXPERTS_PER_STEP)]
            xbs = [jnp.concatenate([_unpack_words(w, 0) for w in xss[j]] + [_unpack_words(w, 1) for w in xss[j]],
                                   axis=-1).astype(BF16) for j in range(EXPERTS_PER_STEP)]
            acts = [jnp.dot(xbs[j], wgu_ref[j], preferred_element_type=F32) for j in range(EXPERTS_PER_STEP)]
            hids = [(_silu(a[:, :EXPERT_HIDDEN]) * a[:, EXPERT_HIDDEN:]).astype(BF16) for a in acts]
            ys = [jnp.dot(hids[j], wd_ref[j], preferred_element_type=F32) for j in range(EXPERTS_PER_STEP)]
            for j in range(EXPERTS_PER_STEP):
                valid = lax.broadcasted_iota(jnp.int32, (EXPERT_TILE, 1), 0) + i * EXPERT_TILE < counts[j]
                yw = _pack_row(ys[j])
                for s in range(PACK_ROWS):
                    gbuf[_token_chunk(r0s[j], s, EXPERT_TILE), :] = jnp.where(
                        valid, yw[:, s * LANES:(s + 1) * LANES], xss[j][s])
            return carry

        lax.fori_loop(0, functools.reduce(jnp.maximum, ntiles), tile, 0)

    @pl.when(k >= nd + ne)
    def _():
        t0 = (k - nd - ne) * MOE_SUB

        def body(i, carry):
            for u in range(ROW_UNROLL):
                t = i * ROW_UNROLL + u
                y1 = gbuf[_token_rows(pos_ref[base + t0 + t]), :]
                y2 = gbuf[_token_rows(pos_ref[base + seq + t0 + t]), :]
                w1 = wts_ref[base + t0 + t]
                w2 = wts_ref[base + seq + t0 + t]
                for half in range(2):
                    stage[pl.ds(pl.multiple_of(t * SUBLANES, SUBLANES) + half * PACK_ROWS, PACK_ROWS), :] = (
                        w1 * _unpack_words(y1, half) + w2 * _unpack_words(y2, half))
            return carry

        lax.fori_loop(0, MOE_SUB // ROW_UNROLL, body, 0)
        for s in range(SUBLANES):
            sl = slice(s * LANES, (s + 1) * LANES)
            o_ref[:, sl] = x_ref[:, sl] + mod_ref[5:6, sl] * stage[_token_chunk(0, s, MOE_SUB, SUBLANES), :]
        if final_norm:
            x2 = o_ref[...]
            ms = jnp.mean(x2 * x2, axis=-1, keepdims=True)
            o_ref[...] = x2 * lax.rsqrt(ms + RMS_EPS) * fg_ref[...]


def _moe(h2r, pos, wts, tbl, wgu, wd, x1, mod, final_g, final_norm):
    batch, seq_len, d = x1.shape
    b, s = batch // MOE_SEQS, seq_len * MOE_SEQS
    h2r = h2r.reshape(b, s * PACK_ROWS, LANES)
    x1 = x1.reshape(b, s, d)
    per_seq = seq_len // MOE_SUB
    nd = s // MOE_SUB
    ne = N_EXPERTS // EXPERTS_PER_STEP
    clip = lambda v, hi: jnp.minimum(jnp.maximum(v, 0), hi)
    tok_in = lambda i, k, *_: (i, clip(k, nd - 1), 0)
    tok_out = lambda i, k, *_: (i, clip(k - nd - ne, nd - 1), 0)
    expert = lambda i, k, *_: (clip(k - nd, ne - 1), 0, 0)
    grid_spec = pltpu.PrefetchScalarGridSpec(
        num_scalar_prefetch=3,
        grid=(b, nd + ne + nd),
        in_specs=[
            pl.BlockSpec((None, MOE_SUB * PACK_ROWS, LANES), tok_in),
            pl.BlockSpec((EXPERTS_PER_STEP, d, 2 * EXPERT_HIDDEN), expert),
            pl.BlockSpec((EXPERTS_PER_STEP, EXPERT_HIDDEN, d), expert),
            pl.BlockSpec((None, MOE_SUB, d), tok_out),
            pl.BlockSpec((None, N_MOD, d),
                         lambda i, k, *_: (i * MOE_SEQS + clip(k - nd - ne, nd - 1) // per_seq, 0, 0)),
            pl.BlockSpec((1, d), lambda i, k, *_: (0, 0)),
        ],
        out_specs=pl.BlockSpec((None, MOE_SUB, d), tok_out),
        scratch_shapes=[
            pltpu.VMEM(((2 * s + EXPERT_TILE) * PACK_ROWS, LANES), jnp.int32),
            pltpu.VMEM((MOE_SUB * SUBLANES, LANES), F32),
        ],
    )
    return pl.pallas_call(
        functools.partial(_moe_kernel, final_norm=final_norm, seq=s),
        grid_spec=grid_spec,
        out_shape=jax.ShapeDtypeStruct((b, s, d), F32),
        compiler_params=pltpu.CompilerParams(
            dimension_semantics=("arbitrary", "arbitrary"), vmem_limit_bytes=MOE_VMEM_LIMIT),
        name="moe",
    )(pos, wts, tbl, h2r, wgu, wd, x1, mod, final_g).reshape(batch, seq_len, d)


def _dispatch_tables(route, counts):
    batch, ncol, seq_len = route.shape
    route = jnp.swapaxes(route.reshape(batch // MOE_SEQS, MOE_SEQS, ncol, seq_len), 1, 2).reshape(
        batch // MOE_SEQS, ncol, MOE_SEQS * seq_len)
    ids = route[:, 0:2, :].astype(jnp.int32)
    ranks = route[:, 4:6, :].astype(jnp.int32)
    cnt = counts[:, 0, :N_EXPERTS].astype(jnp.int32)
    starts = jnp.cumsum(cnt, axis=1) - cnt
    hot = ids[..., None] == jnp.arange(N_EXPERTS, dtype=jnp.int32)
    pos = jnp.sum(jnp.where(hot, starts[:, None, None, :], 0), axis=-1) + ranks
    row0 = (pos * PACK_ROWS).reshape(-1)
    wts = route[:, 2:4, :].reshape(-1)
    tbl = jnp.stack([starts, cnt], axis=-1).reshape(-1)
    return row0, wts, tbl


def _router_weights(rg_w, rg_b, re_w, re_b):
    d = rg_w.shape[0]
    pad = LANES - N_EXPERTS - N_GROUPS
    w = jnp.concatenate([re_w, rg_w, jnp.zeros((d, pad), F32)], axis=1)
    bias = jnp.concatenate([re_b, rg_b, jnp.zeros((pad,), F32)]).reshape(1, LANES)
    w_hi = w.astype(BF16)
    w_lo = (w - w_hi.astype(F32)).astype(BF16)
    return jnp.concatenate([w_hi, w_lo], axis=1), bias


def kernel(x, c, ada_w, ada_b, norm1_g, w_in, pool_w, pool_scale, conv_w, conv_b, conv_ln_g, conv_ln_b,
           w_out, norm2_g, router_group_w, router_group_b, router_expert_w, router_expert_b,
           expert_w_gate, expert_w_up, expert_w_down, final_g):
    depth = ada_w.shape[0]
    b = x.shape[0]
    mods = _adaln(c, ada_w, ada_b).reshape(depth, b, N_MOD, D_MODEL)
    head = lax.broadcasted_iota(jnp.int32, (LN_BLOCK, LN_BLOCK), 0) // CONV_HEAD_DIM
    head_t = lax.broadcasted_iota(jnp.int32, (LN_BLOCK, LN_BLOCK), 1) // CONV_HEAD_DIM
    avg = jnp.where(head == head_t, 1.0 / CONV_HEAD_DIM, 0.0).astype(BF16)
    earlier = (lax.broadcasted_iota(jnp.int32, (TAIL_ROWS, TAIL_ROWS), 1)
               < lax.broadcasted_iota(jnp.int32, (TAIL_ROWS, TAIL_ROWS), 0))
    tril = jnp.where(earlier, 1.0, 0.0).astype(BF16)
    row = lambda v: v.reshape(1, -1)
    for l in range(depth):
        win = jnp.swapaxes(w_in[l].astype(BF16).reshape(D_MODEL, -1, PROJ_PIECE), 0, 1)
        wr, br = _router_weights(router_group_w[l], router_group_b[l],
                                       router_expert_w[l], router_expert_b[l])
        x1, h2r, route, counts, wgu, wd = _mixer(
            x, mods[l], row(norm1_g[l]), win, pool_w[l].astype(BF16), row(pool_scale[l]),
            jnp.broadcast_to(conv_w[l][:, None, :], (CONV_KERNEL, SUBLANES, CONV_WIDTH)),
            row(conv_b[l]), row(conv_ln_g[l]), row(conv_ln_b[l]), avg,
            w_out[l].astype(BF16), row(norm2_g[l]), wr, br, tril,
            l, expert_w_gate, expert_w_up, expert_w_down)
        pos, wts, tbl = _dispatch_tables(route, counts)
        x = _moe(h2r, pos, wts, tbl, wgu, wd, x1, mods[l], row(final_g),
                 final_norm=(l == depth - 1))
    return x
```

```python
import functools
import itertools

import jax
import jax.numpy as jnp
from jax import lax
from jax.experimental import pallas as pl
from jax.experimental.pallas import tpu as pltpu

D_MODEL = 1024
POOL_WIDTH = 512
POOL_GROUPS = 4
POOL_GROUP_DIM = 128
POOL_WINDOWS = (2, 4, 8, 16)
CONV_WIDTH = 512
CONV_HEAD_DIM = 64
CONV_KERNEL = 31
N_GROUPS = 4
EXPERTS_PER_GROUP = 8
N_EXPERTS = 32
EXPERT_HIDDEN = 256
N_MOD = 6
RMS_EPS = 1e-6
LN_EPS = 1e-5

LANES = 128
SUBLANES = 8
PACK_ROWS = 4
POOL_HALO = SUBLANES * len(POOL_WINDOWS)
CONV_HALO = 32
MIX_TILE = 512
CONV_ROWS = 32
LN_BLOCK = 256
WEIGHT_SPLIT = 2
OUT_ROWS = 256
TAIL_ROWS = 128
PROJ_PIECE = 256
MOE_SEQS = 2
DISPATCH_SUB = 1024
MOE_SUB = 512
EXPERTS_PER_STEP = 4
EXPERT_TILE = 144
ROW_UNROLL = 64
VMEM_LIMIT = 56 * 1024 * 1024
MOE_VMEM_LIMIT = 61 * 1024 * 1024

F32 = jnp.float32
BF16 = jnp.bfloat16


def _silu(v):
    return v * jax.nn.sigmoid(v)


def _split_bf16(v):
    hi = v.astype(BF16)
    lo = (v - hi.astype(F32)).astype(BF16)
    return hi, lo


def _adaln_kernel(c_ref, w_ref, b_ref, o_ref):
    cond = _silu(c_ref[...])
    o_ref[...] = jnp.dot(cond, w_ref[...], preferred_element_type=F32,
                         precision=lax.Precision.HIGHEST) + b_ref[...]


def _adaln(c, ada_w, ada_b):
    depth, d, n = ada_w.shape
    b = c.shape[0]
    tn = 2 * D_MODEL
    return pl.pallas_call(
        _adaln_kernel,
        grid=(depth, n // tn),
        in_specs=[
            pl.BlockSpec((b, d), lambda l, j: (0, 0)),
            pl.BlockSpec((None, d, tn), lambda l, j: (l, 0, j)),
            pl.BlockSpec((None, 1, tn), lambda l, j: (l, 0, j)),
        ],
        out_specs=pl.BlockSpec((None, b, tn), lambda l, j: (l, 0, j)),
        out_shape=jax.ShapeDtypeStruct((depth, b, n), F32),
        compiler_params=pltpu.CompilerParams(dimension_semantics=("arbitrary", "arbitrary")),
        name="adaln",
    )(c, ada_w, ada_b.reshape(depth, 1, n))


def _route(logits, tril_ref, cnt_scr):
    t = logits.shape[0]
    lane = lax.broadcasted_iota(jnp.int32, (t, LANES), 1)
    lanef = lane.astype(F32)
    neg = float("-inf")
    big = float(4 * LANES)
    is_group = (lane >= N_EXPERTS) & (lane < N_EXPERTS + N_GROUPS)
    gl = jnp.where(is_group, logits, neg)
    gmax = jnp.max(gl, axis=-1, keepdims=True)
    gidx = jnp.min(jnp.where(gl == gmax, lanef, big), axis=-1, keepdims=True)
    p_group = 1.0 / jnp.sum(jnp.exp(gl - gmax), axis=-1, keepdims=True)
    gsel = gidx.astype(jnp.int32) - N_EXPERTS
    in_group = (lane < N_EXPERTS) & ((lane >> 3) == gsel)
    el = jnp.where(in_group, logits, neg)
    m1 = jnp.max(el, axis=-1, keepdims=True)
    i1 = jnp.min(jnp.where(el == m1, lanef, big), axis=-1, keepdims=True)
    el2 = jnp.where(lanef == i1, neg, el)
    m2 = jnp.max(el2, axis=-1, keepdims=True)
    i2 = jnp.min(jnp.where(el2 == m2, lanef, big), axis=-1, keepdims=True)
    ratio = jnp.exp(m2 - m1)
    w1 = p_group / (1.0 + ratio)
    w2 = p_group * ratio / (1.0 + ratio)
    hot1 = lanef == i1
    hot2 = lanef == i2
    onehot = jnp.where(hot1, 1.0, 0.0) + jnp.where(hot2, 1.0, 0.0)
    before = jnp.dot(tril_ref[...], onehot.astype(BF16), preferred_element_type=F32) + cnt_scr[...]
    rank1 = jnp.sum(jnp.where(hot1, before, 0.0), axis=-1, keepdims=True)
    rank2 = jnp.sum(jnp.where(hot2, before, 0.0), axis=-1, keepdims=True)
    cnt_scr[...] = cnt_scr[...] + jnp.sum(onehot, axis=0, keepdims=True)
    slab = jnp.zeros((t, LANES), F32)
    for col, val in enumerate((i1, i2, w1, w2, rank1, rank2)):
        slab = jnp.where(lane == col, val, slab)
    return slab


def _mixer_kernel(x_ref, xnext_ref, mod_ref, modnext_ref, n1g_ref, win_ref, poolw_ref, pscale_ref, convw_ref,
                  convb_ref, lng_ref, lnb_ref, avg_ref, wout_ref, n2g_ref, wr_ref, br_ref, tril_ref,
                  ewg_ref, ewu_ref, ewd_ref,
                  xo_ref, h2_ref, route_ref, cnt_ref, wgu_ref, wd_ref,
                  uext, pbuf, vsh, cbuf, ycat, hbuf, zbuf, cnt_scr):
    t = MIX_TILE
    i = pl.program_id(0)
    j = pl.program_id(1)

    wgu_ref[:, 0:EXPERT_HIDDEN] = ewg_ref[...].astype(BF16)
    wgu_ref[:, EXPERT_HIDDEN:] = ewu_ref[...].astype(BF16)
    wd_ref[...] = ewd_ref[...].astype(BF16)

    gate1 = mod_ref[2:3, :]
    shift2, scale2 = mod_ref[3:4, :], mod_ref[4:5, :]

    def norm_in(src_ref, m_ref):
        xin = src_ref[...]
        ms = jnp.mean(xin * xin, axis=-1, keepdims=True)
        gain = n1g_ref[...] * (1.0 + m_ref[1:2, :])
        hbuf[...] = (xin * lax.rsqrt(ms + RMS_EPS) * gain + m_ref[0:1, :]).astype(BF16)

    def project(piece):
        zbuf[piece] = jnp.dot(hbuf[...], win_ref[piece], preferred_element_type=F32)

    def projected(col):
        piece, off = divmod(col, PROJ_PIECE)
        return zbuf[piece, :, off:off + LANES]

    n_pieces = win_ref.shape[0]

    @pl.when((i == 0) & (j == 0))
    def _():
        norm_in(x_ref, mod_ref)
        for piece in range(n_pieces):
            project(piece)

    @pl.when((j == 0) & (i % MOE_SEQS == 0))
    def _():
        cnt_scr[...] = jnp.zeros_like(cnt_scr)

    @pl.when(j == 0)
    def _():
        uext[0:POOL_HALO, :] = jnp.zeros((POOL_HALO, POOL_WIDTH), F32)
        vsh[0, 0:CONV_HALO, :] = jnp.zeros((CONV_HALO, CONV_WIDTH), F32)

    @pl.when(j > 0)
    def _():
        uext[0:POOL_HALO, :] = uext[t:t + POOL_HALO, :]
        vsh[0, 0:CONV_HALO, :] = vsh[0, t:t + CONV_HALO, :]

    for p in range(POOL_WIDTH // LANES):
        uext[POOL_HALO:POOL_HALO + t, p * LANES:(p + 1) * LANES] = projected(p * LANES)
    for p in range(CONV_WIDTH // LANES):
        value = projected(POOL_WIDTH + p * LANES)
        gate = projected(POOL_WIDTH + CONV_WIDTH + p * LANES)
        vsh[0, CONV_HALO:CONV_HALO + t, p * LANES:(p + 1) * LANES] = value * jax.nn.sigmoid(gate)

    norm_in(xnext_ref, modnext_ref)
    pieces = iter(range(n_pieces))

    def project_some(n):
        for piece in itertools.islice(pieces, n):
            project(piece)

    pos = j * t + lax.broadcasted_iota(jnp.int32, (t, 1), 0)
    rows = POOL_HALO + t
    for g, w in enumerate(POOL_WINDOWS):
        assert w == 2 << g
        project_some(1)
        half, lo, c0 = w // 2, SUBLANES * (g + 1), g * POOL_GROUP_DIM
        if g == 0:
            level = uext[lo:rows, c0:] + uext[lo - half:rows - half, c0:]
        else:
            level = pbuf[g - 1, lo:rows, c0:] + pbuf[g - 1, lo - half:rows - half, c0:]
        if g + 1 < len(POOL_WINDOWS):
            pbuf[g, lo:rows, c0:] = level
        s = level[POOL_HALO - lo:, 0:POOL_GROUP_DIM]
        cur = uext[POOL_HALO:rows, c0:c0 + POOL_GROUP_DIM]
        inv_count = 1.0 / jnp.minimum(pos + 1, w).astype(F32)
        pooled = s * inv_count - cur
        yp = jnp.dot(pooled.astype(BF16), poolw_ref[g], preferred_element_type=F32)
        ycat[:, c0:c0 + POOL_GROUP_DIM] = (yp * pscale_ref[:, c0:c0 + POOL_GROUP_DIM]).astype(BF16)

    shifted_rows = CONV_HALO + t - SUBLANES
    for r in range(1, SUBLANES):
        if r % 3 == 1:
            project_some(1)
        vsh[r, 0:shifted_rows, :] = vsh[0, r:r + shifted_rows, :]
    project_some(n_pieces)

    def conv_chunk(c, carry):
        r0 = pl.multiple_of(c * CONV_ROWS, CONV_ROWS)
        groups = range(0, CONV_ROWS, SUBLANES)
        accs = [jnp.broadcast_to(convb_ref[...], (SUBLANES, CONV_WIDTH)) for _ in groups]
        for k in range(CONV_KERNEL):
            q, r = divmod(CONV_HALO - (CONV_KERNEL - 1) + k, SUBLANES)
            w = convw_ref[k]
            for n, g in enumerate(groups):
                accs[n] = accs[n] + w * vsh[r, pl.ds(r0 + q * SUBLANES + g, SUBLANES), :]
        for n, g in enumerate(groups):
            cbuf[pl.ds(r0 + g, SUBLANES), :] = accs[n]
        return carry

    lax.fori_loop(0, t // CONV_ROWS, conv_chunk, 0)

    blocks = [slice(r, r + TAIL_ROWS) for r in range(0, t, TAIL_ROWS)]

    for rs in blocks:
        for c in range(CONV_WIDTH // LN_BLOCK):
            c0 = c * LN_BLOCK
            yc = cbuf[rs, c0:c0 + LN_BLOCK]
            hi, lo = _split_bf16(yc)
            mu = (jnp.dot(hi, avg_ref[...], preferred_element_type=F32)
                  + jnp.dot(lo, avg_ref[...], preferred_element_type=F32))
            dlt = yc - mu
            hi, lo = _split_bf16(dlt * dlt)
            var = (jnp.dot(hi, avg_ref[...], preferred_element_type=F32)
                   + jnp.dot(lo, avg_ref[...], preferred_element_type=F32))
            yn = dlt * lax.rsqrt(var + LN_EPS) * lng_ref[:, c0:c0 + LN_BLOCK] + lnb_ref[:, c0:c0 + LN_BLOCK]
            ycat[rs, POOL_WIDTH + c0:POOL_WIDTH + c0 + LN_BLOCK] = _silu(yn).astype(BF16)

    x1s = []
    for r in range(0, t, OUT_ROWS):
        rs = slice(r, r + OUT_ROWS)
        x1 = x_ref[rs, :] + gate1 * jnp.dot(ycat[rs, :], wout_ref[...], preferred_element_type=F32)
        xo_ref[rs, :] = x1
        x1s.extend(x1[q:q + TAIL_ROWS] for q in range(0, OUT_ROWS, TAIL_ROWS))

    h2s = []
    for rs, x1 in zip(blocks, x1s):
        ms2 = jnp.mean(x1 * x1, axis=-1, keepdims=True)
        h2 = x1 * lax.rsqrt(ms2 + RMS_EPS) * (n2g_ref[...] * (1.0 + scale2)) + shift2
        words = _pack_row(h2)
        for s in range(PACK_ROWS):
            h2_ref[_token_chunk(rs.start, s, TAIL_ROWS), :] = words[:, s * LANES:(s + 1) * LANES]
        h2s.append(h2)

    logits = []
    for h2 in h2s:
        hi, lo = _split_bf16(h2)
        prod = (jnp.dot(hi, wr_ref[...], preferred_element_type=F32)
                + jnp.dot(lo, wr_ref[...], preferred_element_type=F32))
        logits.append(prod[:, 0:LANES] + prod[:, LANES:2 * LANES] + br_ref[...])

    for rs, lg in zip(blocks, logits):
        route_ref[:, rs] = _route(lg, tril_ref, cnt_scr).T[0:SUBLANES, :]
    cnt_ref[...] = cnt_scr[...]


def _mixer(x, mod, n1g, win, poolw, pscale, convw, convb, lng, lnb, avg, wout, n2g, wr, br, tril,
           layer, ewg, ewu, ewd):
    b, s, d = x.shape
    t = MIX_TILE
    n_exp, hid = ewg.shape[1], ewg.shape[3]
    assert b * (s // t) == WEIGHT_SPLIT * n_exp
    wshare = lambda i, j: ((i * (s // t) + j) // WEIGHT_SPLIT, (i * (s // t) + j) % WEIGHT_SPLIT, 0)
    assert d == SUBLANES * LANES and win.shape[1:] == (d, PROJ_PIECE)
    nj = s // t
    succ = lambda i, j: (jnp.minimum(i + (j + 1) // nj, b - 1), jnp.where(i + (j + 1) // nj < b, (j + 1) % nj, j))
    const = lambda shape: pl.BlockSpec(shape, lambda i, j: (0,) * len(shape))
    tok = lambda width: pl.BlockSpec((None, t, width), lambda i, j: (i, j, 0))
    return pl.pallas_call(
        _mixer_kernel,
        grid=(b, s // t),
        in_specs=[
            tok(d),
            pl.BlockSpec((None, t, d), lambda i, j: (*succ(i, j), 0)),
            pl.BlockSpec((None, N_MOD, d), lambda i, j: (i, 0, 0)),
            pl.BlockSpec((None, N_MOD, d), lambda i, j: (succ(i, j)[0], 0, 0)),
            const((1, d)),
            const(win.shape), const(poolw.shape), const((1, POOL_WIDTH)),
            const(convw.shape), const((1, CONV_WIDTH)), const((1, CONV_WIDTH)), const((1, CONV_WIDTH)),
            const(avg.shape), const(wout.shape), const((1, d)),
            const(wr.shape), const((1, LANES)), const(tril.shape),
            pl.BlockSpec((None, None, d // WEIGHT_SPLIT, hid), lambda i, j: (layer, *wshare(i, j))),
            pl.BlockSpec((None, None, d // WEIGHT_SPLIT, hid), lambda i, j: (layer, *wshare(i, j))),
            pl.BlockSpec((None, None, hid // WEIGHT_SPLIT, d), lambda i, j: (layer, *wshare(i, j))),
        ],
        out_specs=[
            tok(d),
            pl.BlockSpec((None, t * PACK_ROWS, LANES), lambda i, j: (i, j, 0)),
            pl.BlockSpec((None, SUBLANES, t), lambda i, j: (i, 0, j)),
            pl.BlockSpec((None, 1, LANES), lambda i, j: (i // MOE_SEQS, 0, 0)),
            pl.BlockSpec((None, d // WEIGHT_SPLIT, 2 * hid), wshare),
            pl.BlockSpec((None, hid // WEIGHT_SPLIT, d), wshare),
        ],
        out_shape=[
            jax.ShapeDtypeStruct((b, s, d), F32),
            jax.ShapeDtypeStruct((b, s * PACK_ROWS, LANES), jnp.int32),
            jax.ShapeDtypeStruct((b, SUBLANES, s), F32),
            jax.ShapeDtypeStruct((b // MOE_SEQS, 1, LANES), F32),
            jax.ShapeDtypeStruct((n_exp, d, 2 * hid), BF16),
            jax.ShapeDtypeStruct((n_exp, hid, d), BF16),
        ],
        scratch_shapes=[
            pltpu.VMEM((POOL_HALO + t, POOL_WIDTH), F32),
            pltpu.VMEM((len(POOL_WINDOWS) - 1, POOL_HALO + t, POOL_WIDTH), F32),
            pltpu.VMEM((SUBLANES, CONV_HALO + t, CONV_WIDTH), F32),
            pltpu.VMEM((t, CONV_WIDTH), F32),
            pltpu.VMEM((t, d), BF16),
            pltpu.VMEM((t, d), BF16),
            pltpu.VMEM((win.shape[0], t, PROJ_PIECE), F32),
            pltpu.VMEM((1, LANES), F32),
        ],
        compiler_params=pltpu.CompilerParams(
            dimension_semantics=("arbitrary", "arbitrary"), vmem_limit_bytes=VMEM_LIMIT),
        name="mixer",
    )(x, x, mod, mod, n1g, win, poolw, pscale, convw, convb, lng, lnb, avg, wout, n2g, wr, br, tril,
      ewg, ewu, ewd)


def _pack_row(v):
    half = v.shape[1] // 2
    return pltpu.pack_elementwise([v[:, :half], v[:, half:]], packed_dtype=BF16)


def _unpack_words(w, index):
    return pltpu.unpack_elementwise(w, index=index, packed_dtype=BF16, unpacked_dtype=F32)


def _token_rows(row0, rows=PACK_ROWS):
    return pl.ds(pl.multiple_of(row0, rows), rows)


def _token_chunk(tok0, s, n, rows=PACK_ROWS):
    return pl.ds(tok0 * rows + s, n, stride=rows)


def _moe_kernel(pos_ref, wts_ref, tbl_ref, h_ref, wgu_ref, wd_ref, x_ref, mod_ref, fg_ref, o_ref,
                gbuf, stage, *, final_norm, seq):
    nd = seq // DISPATCH_SUB
    ne = N_EXPERTS // EXPERTS_PER_STEP
    b = pl.program_id(0)
    k = pl.program_id(1)
    base = b * (2 * seq)

    @pl.when(k == 0)
    def _():
        gbuf[pl.ds(2 * seq * PACK_ROWS, EXPERT_TILE * PACK_ROWS), :] = jnp.zeros(
            (EXPERT_TILE * PACK_ROWS, LANES), jnp.int32)

    @pl.when(k < nd)
    def _():
        t0 = k * DISPATCH_SUB

        def body(i, carry):
            for u in range(ROW_UNROLL):
                t = i * ROW_UNROLL + u
                row = h_ref[_token_rows(t * PACK_ROWS), :]
                gbuf[_token_rows(pos_ref[base + t0 + t]), :] = row
                gbuf[_token_rows(pos_ref[base + seq + t0 + t]), :] = row
            return carry

        lax.fori_loop(0, DISPATCH_SUB // ROW_UNROLL, body, 0)

    @pl.when((k >= nd) & (k < nd + ne))
    def _():
        starts, counts, ntiles = [], [], []
        for j in range(EXPERTS_PER_STEP):
            e = (k - nd) * EXPERTS_PER_STEP + j
            starts.append(tbl_ref[(b * N_EXPERTS + e) * 2])
            counts.append(tbl_ref[(b * N_EXPERTS + e) * 2 + 1])
            ntiles.append((counts[j] + EXPERT_TILE - 1) // EXPERT_TILE)

        def tile(i, carry):
            r0s = [jnp.where(i < ntiles[j], starts[j] + i * EXPERT_TILE, 2 * seq) for j in range(EXPERTS_PER_STEP)]
            xss = [[gbuf[_token_chunk(r0s[j], s, EXPERT_TILE), :] for s in range(PACK_ROWS)]
                   for j in range(EXPERTS_PER_STEP)]
            xbs = [jnp.concatenate([_unpack_words(w, 0) for w in xss[j]] + [_unpack_words(w, 1) for w in xss[j]],
                                   axis=-1).astype(BF16) for j in range(EXPERTS_PER_STEP)]
            acts = [jnp.dot(xbs[j], wgu_ref[j], preferred_element_type=F32) for j in range(EXPERTS_PER_STEP)]
            hids = [(_silu(a[:, :EXPERT_HIDDEN]) * a[:, EXPERT_HIDDEN:]).astype(BF16) for a in acts]
            ys = [jnp.dot(hids[j], wd_ref[j], preferred_element_type=F32) for j in range(EXPERTS_PER_STEP)]
            for j in range(EXPERTS_PER_STEP):
                valid = lax.broadcasted_iota(jnp.int32, (EXPERT_TILE, 1), 0) + i * EXPERT_TILE < counts[j]
                yw = _pack_row(ys[j])
                for s in range(PACK_ROWS):
                    gbuf[_token_chunk(r0s[j], s, EXPERT_TILE), :] = jnp.where(
                        valid, yw[:, s * LANES:(s + 1) * LANES], xss[j][s])
            return carry

        lax.fori_loop(0, functools.reduce(jnp.maximum, ntiles), tile, 0)

    @pl.when(k >= nd + ne)
    def _():
        t0 = (k - nd - ne) * MOE_SUB

        def body(i, carry):
            for u in range(ROW_UNROLL):
                t = i * ROW_UNROLL + u
                y1 = gbuf[_token_rows(pos_ref[base + t0 + t]), :]
                y2 = gbuf[_token_rows(pos_ref[base + seq + t0 + t]), :]
                w1 = wts_ref[base + t0 + t]
                w2 = wts_ref[base + seq + t0 + t]
                for half in range(2):
                    stage[pl.ds(pl.multiple_of(t * SUBLANES, SUBLANES) + half * PACK_ROWS, PACK_ROWS), :] = (
                        w1 * _unpack_words(y1, half) + w2 * _unpack_words(y2, half))
            return carry

        lax.fori_loop(0, MOE_SUB // ROW_UNROLL, body, 0)
        for s in range(SUBLANES):
            sl = slice(s * LANES, (s + 1) * LANES)
            o_ref[:, sl] = x_ref[:, sl] + mod_ref[5:6, sl] * stage[_token_chunk(0, s, MOE_SUB, SUBLANES), :]
        if final_norm:
            x2 = o_ref[...]
            ms = jnp.mean(x2 * x2, axis=-1, keepdims=True)
            o_ref[...] = x2 * lax.rsqrt(ms + RMS_EPS) * fg_ref[...]


def _moe(h2r, pos, wts, tbl, wgu, wd, x1, mod, final_g, final_norm):
    batch, seq_len, d = x1.shape
    b, s = batch // MOE_SEQS, seq_len * MOE_SEQS
    h2r = h2r.reshape(b, s * PACK_ROWS, LANES)
    x1 = x1.reshape(b, s, d)
    per_seq = seq_len // MOE_SUB
    nd, nc = s // DISPATCH_SUB, s // MOE_SUB
    ne = N_EXPERTS // EXPERTS_PER_STEP
    clip = lambda v, hi: jnp.minimum(jnp.maximum(v, 0), hi)
    tok_in = lambda i, k, *_: (i, clip(k, nd - 1), 0)
    tok_out = lambda i, k, *_: (i, clip(k - nd - ne, nc - 1), 0)
    expert = lambda i, k, *_: (clip(k - nd, ne - 1), 0, 0)
    grid_spec = pltpu.PrefetchScalarGridSpec(
        num_scalar_prefetch=3,
        grid=(b, nd + ne + nc),
        in_specs=[
            pl.BlockSpec((None, DISPATCH_SUB * PACK_ROWS, LANES), tok_in),
            pl.BlockSpec((EXPERTS_PER_STEP, d, 2 * EXPERT_HIDDEN), expert),
            pl.BlockSpec((EXPERTS_PER_STEP, EXPERT_HIDDEN, d), expert),
            pl.BlockSpec((None, MOE_SUB, d), tok_out),
            pl.BlockSpec((None, N_MOD, d),
                         lambda i, k, *_: (i * MOE_SEQS + clip(k - nd - ne, nc - 1) // per_seq, 0, 0)),
            pl.BlockSpec((1, d), lambda i, k, *_: (0, 0)),
        ],
        out_specs=pl.BlockSpec((None, MOE_SUB, d), tok_out),
        scratch_shapes=[
            pltpu.VMEM(((2 * s + EXPERT_TILE) * PACK_ROWS, LANES), jnp.int32),
            pltpu.VMEM((MOE_SUB * SUBLANES, LANES), F32),
        ],
    )
    return pl.pallas_call(
        functools.partial(_moe_kernel, final_norm=final_norm, seq=s),
        grid_spec=grid_spec,
        out_shape=jax.ShapeDtypeStruct((b, s, d), F32),
        compiler_params=pltpu.CompilerParams(
            dimension_semantics=("arbitrary", "arbitrary"), vmem_limit_bytes=MOE_VMEM_LIMIT),
        name="moe",
    )(pos, wts, tbl, h2r, wgu, wd, x1, mod, final_g).reshape(batch, seq_len, d)


def _dispatch_tables(route, counts):
    batch, ncol, seq_len = route.shape
    route = jnp.swapaxes(route.reshape(batch // MOE_SEQS, MOE_SEQS, ncol, seq_len), 1, 2).reshape(
        batch // MOE_SEQS, ncol, MOE_SEQS * seq_len)
    ids = route[:, 0:2, :].astype(jnp.int32)
    ranks = route[:, 4:6, :].astype(jnp.int32)
    cnt = counts[:, 0, :N_EXPERTS].astype(jnp.int32)
    starts = jnp.cumsum(cnt, axis=1) - cnt
    hot = ids[..., None] == jnp.arange(N_EXPERTS, dtype=jnp.int32)
    pos = jnp.sum(jnp.where(hot, starts[:, None, None, :], 0), axis=-1) + ranks
    row0 = (pos * PACK_ROWS).reshape(-1)
    wts = route[:, 2:4, :].reshape(-1)
    tbl = jnp.stack([starts, cnt], axis=-1).reshape(-1)
    return row0, wts, tbl


def _router_weights(rg_w, rg_b, re_w, re_b):
    d = rg_w.shape[0]
    pad = LANES - N_EXPERTS - N_GROUPS
    w = jnp.concatenate([re_w, rg_w, jnp.zeros((d, pad), F32)], axis=1)
    bias = jnp.concatenate([re_b, rg_b, jnp.zeros((pad,), F32)]).reshape(1, LANES)
    w_hi = w.astype(BF16)
    w_lo = (w - w_hi.astype(F32)).astype(BF16)
    return jnp.concatenate([w_hi, w_lo], axis=1), bias


def kernel(x, c, ada_w, ada_b, norm1_g, w_in, pool_w, pool_scale, conv_w, conv_b, conv_ln_g, conv_ln_b,
           w_out, norm2_g, router_group_w, router_group_b, router_expert_w, router_expert_b,
           expert_w_gate, expert_w_up, expert_w_down, final_g):
    depth = ada_w.shape[0]
    b = x.shape[0]
    mods = _adaln(c, ada_w, ada_b).reshape(depth, b, N_MOD, D_MODEL)
    head = lax.broadcasted_iota(jnp.int32, (LN_BLOCK, LN_BLOCK), 0) // CONV_HEAD_DIM
    head_t = lax.broadcasted_iota(jnp.int32, (LN_BLOCK, LN_BLOCK), 1) // CONV_HEAD_DIM
    avg = jnp.where(head == head_t, 1.0 / CONV_HEAD_DIM, 0.0).astype(BF16)
    earlier = (lax.broadcasted_iota(jnp.int32, (TAIL_ROWS, TAIL_ROWS), 1)
               < lax.broadcasted_iota(jnp.int32, (TAIL_ROWS, TAIL_ROWS), 0))
    tril = jnp.where(earlier, 1.0, 0.0).astype(BF16)
    row = lambda v: v.reshape(1, -1)
    for l in range(depth):
        win = jnp.swapaxes(w_in[l].astype(BF16).reshape(D_MODEL, -1, PROJ_PIECE), 0, 1)
        wr, br = _router_weights(router_group_w[l], router_group_b[l],
                                       router_expert_w[l], router_expert_b[l])
        x1, h2r, route, counts, wgu, wd = _mixer(
            x, mods[l], row(norm1_g[l]), win, pool_w[l].astype(BF16), row(pool_scale[l]),
            jnp.broadcast_to(conv_w[l][:, None, :], (CONV_KERNEL, SUBLANES, CONV_WIDTH)),
            row(conv_b[l]), row(conv_ln_g[l]), row(conv_ln_b[l]), avg,
            w_out[l].astype(BF16), row(norm2_g[l]), wr, br, tril,
            l, expert_w_gate, expert_w_up, expert_w_down)
        pos, wts, tbl = _dispatch_tables(route, counts)
        x = _moe(h2r, pos, wts, tbl, wgu, wd, x1, mods[l], row(final_g),
                 final_norm=(l == depth - 1))
    return x
```

```python
import functools
import itertools

import jax
import jax.numpy as jnp
from jax import lax
from jax.experimental import pallas as pl
from jax.experimental.pallas import tpu as pltpu

D_MODEL = 1024
POOL_WIDTH = 512
POOL_GROUPS = 4
POOL_GROUP_DIM = 128
POOL_WINDOWS = (2, 4, 8, 16)
CONV_WIDTH = 512
CONV_HEAD_DIM = 64
CONV_KERNEL = 31
N_GROUPS = 4
EXPERTS_PER_GROUP = 8
N_EXPERTS = 32
EXPERT_HIDDEN = 256
N_MOD = 6
RMS_EPS = 1e-6
LN_EPS = 1e-5

LANES = 128
SUBLANES = 8
PACK_ROWS = 4
POOL_HALO = SUBLANES * len(POOL_WINDOWS)
CONV_HALO = 32
MIX_TILE = 512
CONV_ROWS = 32
LN_BLOCK = 256
WEIGHT_SPLIT = 2
OUT_ROWS = 256
TAIL_ROWS = 128
PROJ_PIECE = 256
MOE_SEQS = 2
DISPATCH_SUB = 1024
MOE_SUB = 512
EXPERTS_PER_STEP = 4
EXPERT_TILE = 192
ROW_UNROLL = 128
VMEM_LIMIT = 56 * 1024 * 1024
MOE_VMEM_LIMIT = 61 * 1024 * 1024

F32 = jnp.float32
BF16 = jnp.bfloat16


def _silu(v):
    return v * jax.nn.sigmoid(v)


def _split_bf16(v):
    hi = v.astype(BF16)
    lo = (v - hi.astype(F32)).astype(BF16)
    return hi, lo


def _adaln_kernel(c_ref, w_ref, b_ref, o_ref):
    cond = _silu(c_ref[...])
    o_ref[...] = jnp.dot(cond, w_ref[...], preferred_element_type=F32,
                         precision=lax.Precision.HIGHEST) + b_ref[...]


def _adaln(c, ada_w, ada_b):
    depth, d, n = ada_w.shape
    b = c.shape[0]
    tn = 2 * D_MODEL
    return pl.pallas_call(
        _adaln_kernel,
        grid=(depth, n // tn),
        in_specs=[
            pl.BlockSpec((b, d), lambda l, j: (0, 0)),
            pl.BlockSpec((None, d, tn), lambda l, j: (l, 0, j)),
            pl.BlockSpec((None, 1, tn), lambda l, j: (l, 0, j)),
        ],
        out_specs=pl.BlockSpec((None, b, tn), lambda l, j: (l, 0, j)),
        out_shape=jax.ShapeDtypeStruct((depth, b, n), F32),
        compiler_params=pltpu.CompilerParams(dimension_semantics=("arbitrary", "arbitrary")),
        name="adaln",
    )(c, ada_w, ada_b.reshape(depth, 1, n))


def _route(logits, tril_ref, cnt_scr):
    t = logits.shape[0]
    lane = lax.broadcasted_iota(jnp.int32, (t, LANES), 1)
    lanef = lane.astype(F32)
    neg = float("-inf")
    big = float(4 * LANES)
    is_group = (lane >= N_EXPERTS) & (lane < N_EXPERTS + N_GROUPS)
    gl = jnp.where(is_group, logits, neg)
    gmax = jnp.max(gl, axis=-1, keepdims=True)
    gidx = jnp.min(jnp.where(gl == gmax, lanef, big), axis=-1, keepdims=True)
    p_group = 1.0 / jnp.sum(jnp.exp(gl - gmax), axis=-1, keepdims=True)
    gsel = gidx.astype(jnp.int32) - N_EXPERTS
    in_group = (lane < N_EXPERTS) & ((lane >> 3) == gsel)
    el = jnp.where(in_group, logits, neg)
    m1 = jnp.max(el, axis=-1, keepdims=True)
    i1 = jnp.min(jnp.where(el == m1, lanef, big), axis=-1, keepdims=True)
    el2 = jnp.where(lanef == i1, neg, el)
    m2 = jnp.max(el2, axis=-1, keepdims=True)
    i2 = jnp.min(jnp.where(el2 == m2, lanef, big), axis=-1, keepdims=True)
    ratio = jnp.exp(m2 - m1)
    w1 = p_group / (1.0 + ratio)
    w2 = p_group * ratio / (1.0 + ratio)
    hot1 = lanef == i1
    hot2 = lanef == i2
    onehot = jnp.where(hot1, 1.0, 0.0) + jnp.where(hot2, 1.0, 0.0)
    before = jnp.dot(tril_ref[...], onehot.astype(BF16), preferred_element_type=F32) + cnt_scr[...]
    rank1 = jnp.sum(jnp.where(hot1, before, 0.0), axis=-1, keepdims=True)
    rank2 = jnp.sum(jnp.where(hot2, before, 0.0), axis=-1, keepdims=True)
    cnt_scr[...] = cnt_scr[...] + jnp.sum(onehot, axis=0, keepdims=True)
    slab = jnp.zeros((t, LANES), F32)
    for col, val in enumerate((i1, i2, w1, w2, rank1, rank2)):
        slab = jnp.where(lane == col, val, slab)
    return slab


def _mixer_kernel(x_ref, xnext_ref, mod_ref, modnext_ref, n1g_ref, win_ref, poolw_ref, pscale_ref, convw_ref,
                  convb_ref, lng_ref, lnb_ref, avg_ref, wout_ref, n2g_ref, wr_ref, br_ref, tril_ref,
                  ewg_ref, ewu_ref, ewd_ref,
                  xo_ref, h2_ref, route_ref, cnt_ref, wgu_ref, wd_ref,
                  uext, pbuf, vsh, cbuf, ycat, hbuf, zbuf, cnt_scr):
    t = MIX_TILE
    i = pl.program_id(0)
    j = pl.program_id(1)

    wgu_ref[:, 0:EXPERT_HIDDEN] = ewg_ref[...].astype(BF16)
    wgu_ref[:, EXPERT_HIDDEN:] = ewu_ref[...].astype(BF16)
    wd_ref[...] = ewd_ref[...].astype(BF16)

    gate1 = mod_ref[2:3, :]
    shift2, scale2 = mod_ref[3:4, :], mod_ref[4:5, :]

    def norm_in(src_ref, m_ref):
        xin = src_ref[...]
        ms = jnp.mean(xin * xin, axis=-1, keepdims=True)
        gain = n1g_ref[...] * (1.0 + m_ref[1:2, :])
        hbuf[...] = (xin * lax.rsqrt(ms + RMS_EPS) * gain + m_ref[0:1, :]).astype(BF16)

    def project(piece):
        zbuf[piece] = jnp.dot(hbuf[...], win_ref[piece], preferred_element_type=F32)

    def projected(col):
        piece, off = divmod(col, PROJ_PIECE)
        return zbuf[piece, :, off:off + LANES]

    n_pieces = win_ref.shape[0]

    @pl.when((i == 0) & (j == 0))
    def _():
        norm_in(x_ref, mod_ref)
        for piece in range(n_pieces):
            project(piece)

    @pl.when((j == 0) & (i % MOE_SEQS == 0))
    def _():
        cnt_scr[...] = jnp.zeros_like(cnt_scr)

    @pl.when(j == 0)
    def _():
        uext[0:POOL_HALO, :] = jnp.zeros((POOL_HALO, POOL_WIDTH), F32)
        vsh[0, 0:CONV_HALO, :] = jnp.zeros((CONV_HALO, CONV_WIDTH), F32)

    @pl.when(j > 0)
    def _():
        uext[0:POOL_HALO, :] = uext[t:t + POOL_HALO, :]
        vsh[0, 0:CONV_HALO, :] = vsh[0, t:t + CONV_HALO, :]

    for p in range(POOL_WIDTH // LANES):
        uext[POOL_HALO:POOL_HALO + t, p * LANES:(p + 1) * LANES] = projected(p * LANES)
    for p in range(CONV_WIDTH // LANES):
        value = projected(POOL_WIDTH + p * LANES)
        gate = projected(POOL_WIDTH + CONV_WIDTH + p * LANES)
        vsh[0, CONV_HALO:CONV_HALO + t, p * LANES:(p + 1) * LANES] = value * jax.nn.sigmoid(gate)

    norm_in(xnext_ref, modnext_ref)
    pieces = iter(range(n_pieces))

    def project_some(n):
        for piece in itertools.islice(pieces, n):
            project(piece)

    pos = j * t + lax.broadcasted_iota(jnp.int32, (t, 1), 0)
    rows = POOL_HALO + t
    for g, w in enumerate(POOL_WINDOWS):
        assert w == 2 << g
        project_some(1)
        half, lo, c0 = w // 2, SUBLANES * (g + 1), g * POOL_GROUP_DIM
        if g == 0:
            level = uext[lo:rows, c0:] + uext[lo - half:rows - half, c0:]
        else:
            level = pbuf[g - 1, lo:rows, c0:] + pbuf[g - 1, lo - half:rows - half, c0:]
        if g + 1 < len(POOL_WINDOWS):
            pbuf[g, lo:rows, c0:] = level
        s = level[POOL_HALO - lo:, 0:POOL_GROUP_DIM]
        cur = uext[POOL_HALO:rows, c0:c0 + POOL_GROUP_DIM]
        inv_count = 1.0 / jnp.minimum(pos + 1, w).astype(F32)
        pooled = s * inv_count - cur
        yp = jnp.dot(pooled.astype(BF16), poolw_ref[g], preferred_element_type=F32)
        ycat[:, c0:c0 + POOL_GROUP_DIM] = (yp * pscale_ref[:, c0:c0 + POOL_GROUP_DIM]).astype(BF16)

    shifted_rows = CONV_HALO + t - SUBLANES
    for r in range(1, SUBLANES):
        if r % 3 == 1:
            project_some(1)
        vsh[r, 0:shifted_rows, :] = vsh[0, r:r + shifted_rows, :]
    project_some(n_pieces)

    def conv_chunk(c, carry):
        r0 = pl.multiple_of(c * CONV_ROWS, CONV_ROWS)
        groups = range(0, CONV_ROWS, SUBLANES)
        accs = [jnp.broadcast_to(convb_ref[...], (SUBLANES, CONV_WIDTH)) for _ in groups]
        for k in range(CONV_KERNEL):
            q, r = divmod(CONV_HALO - (CONV_KERNEL - 1) + k, SUBLANES)
            w = convw_ref[k]
            for n, g in enumerate(groups):
                accs[n] = accs[n] + w * vsh[r, pl.ds(r0 + q * SUBLANES + g, SUBLANES), :]
        for n, g in enumerate(groups):
            cbuf[pl.ds(r0 + g, SUBLANES), :] = accs[n]
        return carry

    lax.fori_loop(0, t // CONV_ROWS, conv_chunk, 0)

    blocks = [slice(r, r + TAIL_ROWS) for r in range(0, t, TAIL_ROWS)]

    for rs in blocks:
        for c in range(CONV_WIDTH // LN_BLOCK):
            c0 = c * LN_BLOCK
            yc = cbuf[rs, c0:c0 + LN_BLOCK]
            hi, lo = _split_bf16(yc)
            mu = (jnp.dot(hi, avg_ref[...], preferred_element_type=F32)
                  + jnp.dot(lo, avg_ref[...], preferred_element_type=F32))
            dlt = yc - mu
            hi, lo = _split_bf16(dlt * dlt)
            var = (jnp.dot(hi, avg_ref[...], preferred_element_type=F32)
                   + jnp.dot(lo, avg_ref[...], preferred_element_type=F32))
            yn = dlt * lax.rsqrt(var + LN_EPS) * lng_ref[:, c0:c0 + LN_BLOCK] + lnb_ref[:, c0:c0 + LN_BLOCK]
            ycat[rs, POOL_WIDTH + c0:POOL_WIDTH + c0 + LN_BLOCK] = _silu(yn).astype(BF16)

    x1s = []
    for r in range(0, t, OUT_ROWS):
        rs = slice(r, r + OUT_ROWS)
        x1 = x_ref[rs, :] + gate1 * jnp.dot(ycat[rs, :], wout_ref[...], preferred_element_type=F32)
        xo_ref[rs, :] = x1
        x1s.extend(x1[q:q + TAIL_ROWS] for q in range(0, OUT_ROWS, TAIL_ROWS))

    h2s = []
    for rs, x1 in zip(blocks, x1s):
        ms2 = jnp.mean(x1 * x1, axis=-1, keepdims=True)
        h2 = x1 * lax.rsqrt(ms2 + RMS_EPS) * (n2g_ref[...] * (1.0 + scale2)) + shift2
        words = _pack_row(h2)
        for s in range(PACK_ROWS):
            h2_ref[_token_chunk(rs.start, s, TAIL_ROWS), :] = words[:, s * LANES:(s + 1) * LANES]
        h2s.append(h2)

    logits = []
    for h2 in h2s:
        hi, lo = _split_bf16(h2)
        prod = (jnp.dot(hi, wr_ref[...], preferred_element_type=F32)
                + jnp.dot(lo, wr_ref[...], preferred_element_type=F32))
        logits.append(prod[:, 0:LANES] + prod[:, LANES:2 * LANES] + br_ref[...])

    for rs, lg in zip(blocks, logits):
        route_ref[:, rs] = _route(lg, tril_ref, cnt_scr).T[0:SUBLANES, :]
    cnt_ref[...] = cnt_scr[...]


def _mixer(x, mod, n1g, win, poolw, pscale, convw, convb, lng, lnb, avg, wout, n2g, wr, br, tril,
           layer, ewg, ewu, ewd):
    b, s, d = x.shape
    t = MIX_TILE
    n_exp, hid = ewg.shape[1], ewg.shape[3]
    assert b * (s // t) == WEIGHT_SPLIT * n_exp
    wshare = lambda i, j: ((i * (s // t) + j) // WEIGHT_SPLIT, (i * (s // t) + j) % WEIGHT_SPLIT, 0)
    assert d == SUBLANES * LANES and win.shape[1:] == (d, PROJ_PIECE)
    nj = s // t
    succ = lambda i, j: (jnp.minimum(i + (j + 1) // nj, b - 1), jnp.where(i + (j + 1) // nj < b, (j + 1) % nj, j))
    const = lambda shape: pl.BlockSpec(shape, lambda i, j: (0,) * len(shape))
    tok = lambda width: pl.BlockSpec((None, t, width), lambda i, j: (i, j, 0))
    return pl.pallas_call(
        _mixer_kernel,
        grid=(b, s // t),
        in_specs=[
            tok(d),
            pl.BlockSpec((None, t, d), lambda i, j: (*succ(i, j), 0)),
            pl.BlockSpec((None, N_MOD, d), lambda i, j: (i, 0, 0)),
            pl.BlockSpec((None, N_MOD, d), lambda i, j: (succ(i, j)[0], 0, 0)),
            const((1, d)),
            const(win.shape), const(poolw.shape), const((1, POOL_WIDTH)),
            const(convw.shape), const((1, CONV_WIDTH)), const((1, CONV_WIDTH)), const((1, CONV_WIDTH)),
            const(avg.shape), const(wout.shape), const((1, d)),
            const(wr.shape), const((1, LANES)), const(tril.shape),
            pl.BlockSpec((None, None, d // WEIGHT_SPLIT, hid), lambda i, j: (layer, *wshare(i, j))),
            pl.BlockSpec((None, None, d // WEIGHT_SPLIT, hid), lambda i, j: (layer, *wshare(i, j))),
            pl.BlockSpec((None, None, hid // WEIGHT_SPLIT, d), lambda i, j: (layer, *wshare(i, j))),
        ],
        out_specs=[
            tok(d),
            pl.BlockSpec((None, t * PACK_ROWS, LANES), lambda i, j: (i, j, 0)),
            pl.BlockSpec((None, SUBLANES, t), lambda i, j: (i, 0, j)),
            pl.BlockSpec((None, 1, LANES), lambda i, j: (i // MOE_SEQS, 0, 0)),
            pl.BlockSpec((None, d // WEIGHT_SPLIT, 2 * hid), wshare),
            pl.BlockSpec((None, hid // WEIGHT_SPLIT, d), wshare),
        ],
        out_shape=[
            jax.ShapeDtypeStruct((b, s, d), F32),
            jax.ShapeDtypeStruct((b, s * PACK_ROWS, LANES), jnp.int32),
            jax.ShapeDtypeStruct((b, SUBLANES, s), F32),
            jax.ShapeDtypeStruct((b // MOE_SEQS, 1, LANES), F32),
            jax.ShapeDtypeStruct((n_exp, d, 2 * hid), BF16),
            jax.ShapeDtypeStruct((n_exp, hid, d), BF16),
        ],
        scratch_shapes=[
            pltpu.VMEM((POOL_HALO + t, POOL_WIDTH), F32),
            pltpu.VMEM((len(POOL_WINDOWS) - 1, POOL_HALO + t, POOL_WIDTH), F32),
            pltpu.VMEM((SUBLANES, CONV_HALO + t, CONV_WIDTH), F32),
            pltpu.VMEM((t, CONV_WIDTH), F32),
            pltpu.VMEM((t, d), BF16),
            pltpu.VMEM((t, d), BF16),
            pltpu.VMEM((win.shape[0], t, PROJ_PIECE), F32),
            pltpu.VMEM((1, LANES), F32),
        ],
        compiler_params=pltpu.CompilerParams(
            dimension_semantics=("arbitrary", "arbitrary"), vmem_limit_bytes=VMEM_LIMIT),
        name="mixer",
    )(x, x, mod, mod, n1g, win, poolw, pscale, convw, convb, lng, lnb, avg, wout, n2g, wr, br, tril,
      ewg, ewu, ewd)


def _pack_row(v):
    half = v.shape[1] // 2
    return pltpu.pack_elementwise([v[:, :half], v[:, half:]], packed_dtype=BF16)


def _unpack_words(w, index):
    return pltpu.unpack_elementwise(w, index=index, packed_dtype=BF16, unpacked_dtype=F32)


def _token_rows(row0, rows=PACK_ROWS):
    return pl.ds(pl.multiple_of(row0, rows), rows)


def _token_chunk(tok0, s, n, rows=PACK_ROWS):
    return pl.ds(tok0 * rows + s, n, stride=rows)


def _moe_kernel(pos_ref, wts_ref, tbl_ref, h_ref, wgu_ref, wd_ref, x_ref, mod_ref, fg_ref, o_ref,
                gbuf, stage, *, final_norm, seq):
    nd = seq // DISPATCH_SUB
    ne = N_EXPERTS // EXPERTS_PER_STEP
    b = pl.program_id(0)
    k = pl.program_id(1)
    base = b * (2 * seq)

    @pl.when(k == 0)
    def _():
        gbuf[pl.ds(2 * seq * PACK_ROWS, EXPERT_TILE * PACK_ROWS), :] = jnp.zeros(
            (EXPERT_TILE * PACK_ROWS, LANES), jnp.int32)

    @pl.when(k < nd)
    def _():
        t0 = k * DISPATCH_SUB

        def body(i, carry):
            for u in range(ROW_UNROLL):
                t = i * ROW_UNROLL + u
                row = h_ref[_token_rows(t * PACK_ROWS), :]
                gbuf[_token_rows(pos_ref[base + t0 + t]), :] = row
                gbuf[_token_rows(pos_ref[base + seq + t0 + t]), :] = row
            return carry

        lax.fori_loop(0, DISPATCH_SUB // ROW_UNROLL, body, 0)

    @pl.when((k >= nd) & (k < nd + ne))
    def _():
        starts, counts, ntiles = [], [], []
        for j in range(EXPERTS_PER_STEP):
            e = (k - nd) * EXPERTS_PER_STEP + j
            starts.append(tbl_ref[(b * N_EXPERTS + e) * 2])
            counts.append(tbl_ref[(b * N_EXPERTS + e) * 2 + 1])
            ntiles.append((counts[j] + EXPERT_TILE - 1) // EXPERT_TILE)

        def tile(i, carry):
            r0s = [jnp.where(i < ntiles[j], starts[j] + i * EXPERT_TILE, 2 * seq) for j in range(EXPERTS_PER_STEP)]
            xss = [[gbuf[_token_chunk(r0s[j], s, EXPERT_TILE), :] for s in range(PACK_ROWS)]
                   for j in range(EXPERTS_PER_STEP)]
            xbs = [jnp.concatenate([_unpack_words(w, 0) for w in xss[j]] + [_unpack_words(w, 1) for w in xss[j]],
                                   axis=-1).astype(BF16) for j in range(EXPERTS_PER_STEP)]
            acts = [jnp.dot(xbs[j], wgu_ref[j], preferred_element_type=F32) for j in range(EXPERTS_PER_STEP)]
            hids = [(_silu(a[:, :EXPERT_HIDDEN]) * a[:, EXPERT_HIDDEN:]).astype(BF16) for a in acts]
            ys = [jnp.dot(hids[j], wd_ref[j], preferred_element_type=F32) for j in range(EXPERTS_PER_STEP)]
            for j in range(EXPERTS_PER_STEP):
                valid = lax.broadcasted_iota(jnp.int32, (EXPERT_TILE, 1), 0) + i * EXPERT_TILE < counts[j]
                yw = _pack_row(ys[j])
                for s in range(PACK_ROWS):
                    gbuf[_token_chunk(r0s[j], s, EXPERT_TILE), :] = jnp.where(
                        valid, yw[:, s * LANES:(s + 1) * LANES], xss[j][s])
            return carry

        lax.fori_loop(0, functools.reduce(jnp.maximum, ntiles), tile, 0)

    @pl.when(k >= nd + ne)
    def _():
        t0 = (k - nd - ne) * MOE_SUB

        def body(i, carry):
            for u in range(ROW_UNROLL):
                t = i * ROW_UNROLL + u
                y1 = gbuf[_token_rows(pos_ref[base + t0 + t]), :]
                y2 = gbuf[_token_rows(pos_ref[base + seq + t0 + t]), :]
                w1 = wts_ref[base + t0 + t]
                w2 = wts_ref[base + seq + t0 + t]
                for half in range(2):
                    stage[pl.ds(pl.multiple_of(t * SUBLANES, SUBLANES) + half * PACK_ROWS, PACK_ROWS), :] = (
                        w1 * _unpack_words(y1, half) + w2 * _unpack_words(y2, half))
            return carry

        lax.fori_loop(0, MOE_SUB // ROW_UNROLL, body, 0)
        for s in range(SUBLANES):
            sl = slice(s * LANES, (s + 1) * LANES)
            o_ref[:, sl] = x_ref[:, sl] + mod_ref[5:6, sl] * stage[_token_chunk(0, s, MOE_SUB, SUBLANES), :]
        if final_norm:
            x2 = o_ref[...]
            ms = jnp.mean(x2 * x2, axis=-1, keepdims=True)
            o_ref[...] = x2 * lax.rsqrt(ms + RMS_EPS) * fg_ref[...]


def _moe(h2r, pos, wts, tbl, wgu, wd, x1, mod, final_g, final_norm):
    batch, seq_len, d = x1.shape
    b, s = batch // MOE_SEQS, seq_len * MOE_SEQS
    h2r = h2r.reshape(b, s * PACK_ROWS, LANES)
    x1 = x1.reshape(b, s, d)
    per_seq = seq_len // MOE_SUB
    nd, nc = s // DISPATCH_SUB, s // MOE_SUB
    ne = N_EXPERTS // EXPERTS_PER_STEP
    clip = lambda v, hi: jnp.minimum(jnp.maximum(v, 0), hi)
    tok_in = lambda i, k, *_: (i, clip(k, nd - 1), 0)
    tok_out = lambda i, k, *_: (i, clip(k - nd - ne, nc - 1), 0)
    expert = lambda i, k, *_: (clip(k - nd, ne - 1), 0, 0)
    grid_spec = pltpu.PrefetchScalarGridSpec(
        num_scalar_prefetch=3,
        grid=(b, nd + ne + nc),
        in_specs=[
            pl.BlockSpec((None, DISPATCH_SUB * PACK_ROWS, LANES), tok_in),
            pl.BlockSpec((EXPERTS_PER_STEP, d, 2 * EXPERT_HIDDEN), expert),
            pl.BlockSpec((EXPERTS_PER_STEP, EXPERT_HIDDEN, d), expert),
            pl.BlockSpec((None, MOE_SUB, d), tok_out),
            pl.BlockSpec((None, N_MOD, d),
                         lambda i, k, *_: (i * MOE_SEQS + clip(k - nd - ne, nc - 1) // per_seq, 0, 0)),
            pl.BlockSpec((1, d), lambda i, k, *_: (0, 0)),
        ],
        out_specs=pl.BlockSpec((None, MOE_SUB, d), tok_out),
        scratch_shapes=[
            pltpu.VMEM(((2 * s + EXPERT_TILE) * PACK_ROWS, LANES), jnp.int32),
            pltpu.VMEM((MOE_SUB * SUBLANES, LANES), F32),
        ],
    )
    return pl.pallas_call(
        functools.partial(_moe_kernel, final_norm=final_norm, seq=s),
        grid_spec=grid_spec,
        out_shape=jax.ShapeDtypeStruct((b, s, d), F32),
        compiler_params=pltpu.CompilerParams(
            dimension_semantics=("arbitrary", "arbitrary"), vmem_limit_bytes=MOE_VMEM_LIMIT),
        name="moe",
    )(pos, wts, tbl, h2r, wgu, wd, x1, mod, final_g).reshape(batch, seq_len, d)


def _dispatch_tables(route, counts):
    batch, ncol, seq_len = route.shape
    route = jnp.swapaxes(route.reshape(batch // MOE_SEQS, MOE_SEQS, ncol, seq_len), 1, 2).reshape(
        batch // MOE_SEQS, ncol, MOE_SEQS * seq_len)
    ids = route[:, 0:2, :].astype(jnp.int32)
    ranks = route[:, 4:6, :].astype(jnp.int32)
    cnt = counts[:, 0, :N_EXPERTS].astype(jnp.int32)
    starts = jnp.cumsum(cnt, axis=1) - cnt
    hot = ids[..., None] == jnp.arange(N_EXPERTS, dtype=jnp.int32)
    pos = jnp.sum(jnp.where(hot, starts[:, None, None, :], 0), axis=-1) + ranks
    row0 = (pos * PACK_ROWS).reshape(-1)
    wts = route[:, 2:4, :].reshape(-1)
    tbl = jnp.stack([starts, cnt], axis=-1).reshape(-1)
    return row0, wts, tbl


def _router_weights(rg_w, rg_b, re_w, re_b):
    d = rg_w.shape[0]
    pad = LANES - N_EXPERTS - N_GROUPS
    w = jnp.concatenate([re_w, rg_w, jnp.zeros((d, pad), F32)], axis=1)
    bias = jnp.concatenate([re_b, rg_b, jnp.zeros((pad,), F32)]).reshape(1, LANES)
    w_hi = w.astype(BF16)
    w_lo = (w - w_hi.astype(F32)).astype(BF16)
    return jnp.concatenate([w_hi, w_lo], axis=1), bias


def kernel(x, c, ada_w, ada_b, norm1_g, w_in, pool_w, pool_scale, conv_w, conv_b, conv_ln_g, conv_ln_b,
           w_out, norm2_g, router_group_w, router_group_b, router_expert_w, router_expert_b,
           expert_w_gate, expert_w_up, expert_w_down, final_g):
    depth = ada_w.shape[0]
    b = x.shape[0]
    mods = _adaln(c, ada_w, ada_b).reshape(depth, b, N_MOD, D_MODEL)
    head = lax.broadcasted_iota(jnp.int32, (LN_BLOCK, LN_BLOCK), 0) // CONV_HEAD_DIM
    head_t = lax.broadcasted_iota(jnp.int32, (LN_BLOCK, LN_BLOCK), 1) // CONV_HEAD_DIM
    avg = jnp.where(head == head_t, 1.0 / CONV_HEAD_DIM, 0.0).astype(BF16)
    earlier = (lax.broadcasted_iota(jnp.int32, (TAIL_ROWS, TAIL_ROWS), 1)
               < lax.broadcasted_iota(jnp.int32, (TAIL_ROWS, TAIL_ROWS), 0))
    tril = jnp.where(earlier, 1.0, 0.0).astype(BF16)
    row = lambda v: v.reshape(1, -1)
    for l in range(depth):
        win = jnp.swapaxes(w_in[l].astype(BF16).reshape(D_MODEL, -1, PROJ_PIECE), 0, 1)
        wr, br = _router_weights(router_group_w[l], router_group_b[l],
                                       router_expert_w[l], router_expert_b[l])
        x1, h2r, route, counts, wgu, wd = _mixer(
            x, mods[l], row(norm1_g[l]), win, pool_w[l].astype(BF16), row(pool_scale[l]),
            jnp.broadcast_to(conv_w[l][:, None, :], (CONV_KERNEL, SUBLANES, CONV_WIDTH)),
            row(conv_b[l]), row(conv_ln_g[l]), row(conv_ln_b[l]), avg,
            w_out[l].astype(BF16), row(norm2_g[l]), wr, br, tril,
            l, expert_w_gate, expert_w_up, expert_w_down)
        pos, wts, tbl = _dispatch_tables(route, counts)
        x = _moe(h2r, pos, wts, tbl, wgu, wd, x1, mods[l], row(final_g),
                 final_norm=(l == depth - 1))
    return x
```
